```python
import jax, jax.numpy as jnp
from jax import lax
import numpy as np

D_MODEL = 2048
BATCH = 2
SEQ = 16384
DEPTH = 1

MIX_WIDTH = D_MODEL
POOL_WIDTH = MIX_WIDTH // 2
POOL_WINDOWS = (2, 4, 8, 16)
POOL_GROUP = POOL_WIDTH // len(POOL_WINDOWS)
ATTN_WIDTH = MIX_WIDTH - POOL_WIDTH
HEAD_DIM = 128
N_HEADS = ATTN_WIDTH // HEAD_DIM
DILATED_PATTERNS = ((128, 1), (512, 4), (2048, 16))
MAX_WINDOW = max(w for w, _ in DILATED_PATTERNS)
Q_BLOCK = 128
IN_WIDTH = POOL_WIDTH + 3 * ATTN_WIDTH
D_FF = 4 * D_MODEL
NORM_EPS = 1e-6

kernel_name = "hybrid_pool_dilated_attn_block"


def rmsnorm(x, g):
    xf = x.astype(jnp.float32)
    y = xf * lax.rsqrt(jnp.mean(xf * xf, axis=-1, keepdims=True) + NORM_EPS)
    return (y * g.astype(jnp.float32)).astype(x.dtype)


def alibi_slopes():
    return 2.0 ** (-8.0 * jnp.arange(1, N_HEADS + 1, dtype=jnp.float32) / N_HEADS)


def pool_mixer(u, pool_w, pool_scale):
    B, S, _ = u.shape
    uf = u.astype(jnp.float32)
    cs = jnp.cumsum(uf, axis=1)
    pos = jnp.arange(1, S + 1, dtype=jnp.float32)[None, :, None]
    diffs = []
    for g, w in enumerate(POOL_WINDOWS):
        c = cs[..., g * POOL_GROUP:(g + 1) * POOL_GROUP]
        trailing = c - jnp.pad(c, ((0, 0), (w, 0), (0, 0)))[:, :S]
        mean = trailing / jnp.minimum(pos, float(w))
        diffs.append(mean - uf[..., g * POOL_GROUP:(g + 1) * POOL_GROUP])
    d = jnp.stack(diffs, axis=2).astype(u.dtype)
    y = jnp.einsum('bsgc,gcf->bsgf', d, pool_w).reshape(B, S, POOL_WIDTH)
    return y * pool_scale


def dilated_attention(q, k, v, slopes):
    B, S, H, Dh = q.shape
    scale = Dh ** -0.5
    kp = jnp.pad(k, ((0, 0), (MAX_WINDOW, 0), (0, 0), (0, 0)))
    vp = jnp.pad(v, ((0, 0), (MAX_WINDOW, 0), (0, 0), (0, 0)))
    qi = jnp.arange(Q_BLOCK)

    def block(b0):
        qb = lax.dynamic_slice_in_dim(q, b0, Q_BLOCK, axis=1).astype(jnp.float32) * scale
        kwin = lax.dynamic_slice_in_dim(kp, b0, Q_BLOCK + MAX_WINDOW, axis=1)
        vwin = lax.dynamic_slice_in_dim(vp, b0, Q_BLOCK + MAX_WINDOW, axis=1)
        ms, dens, nums = [], [], []
        for window, dil in DILATED_PATTERNS:
            dist = jnp.arange(window // dil + 1) * dil
            idx = qi[:, None] + MAX_WINDOW - dist[None, :]
            valid = (b0 + qi)[:, None] >= dist[None, :]
            kg = jnp.take(kwin, idx, axis=1).astype(jnp.float32)
            vg = jnp.take(vwin, idx, axis=1).astype(jnp.float32)
            s = jnp.einsum('bqhd,bqjhd->bqhj', qb, kg)
            s = s - slopes[:, None] * dist.astype(jnp.float32)[None, :]
            s = jnp.where(valid[None, :, None, :], s, -jnp.inf)
            m = jnp.max(s, axis=-1)
            p = jnp.exp(s - m[..., None])
            ms.append(m)
            dens.append(jnp.sum(p, axis=-1))
            nums.append(jnp.einsum('bqhj,bqjhd->bqhd', p, vg))
        m_all = jnp.stack(ms)
        w = jnp.exp(m_all - jnp.max(m_all, axis=0))
        den = jnp.sum(w * jnp.stack(dens), axis=0)
        num = jnp.sum(w[..., None] * jnp.stack(nums), axis=0)
        return num / den[..., None]

    starts = jnp.arange(S // Q_BLOCK) * Q_BLOCK
    out = lax.map(block, starts)
    return out.transpose(1, 0, 2, 3, 4).reshape(B, S, H * Dh).astype(q.dtype)


def setup_inputs(seed: int = 0) -> dict:
    key = jax.random.key(seed)
    ks = jax.random.split(key, 12)
    f32 = jnp.float32
    x = jax.random.normal(ks[0], (BATCH, SEQ, D_MODEL), f32)
    norm_mix_g = 1.0 + 0.05 * jax.random.normal(ks[1], (DEPTH, D_MODEL), f32)
    w_in = jax.random.normal(ks[2], (DEPTH, D_MODEL, IN_WIDTH), f32) * D_MODEL ** -0.5
    pool_w = jax.random.normal(ks[3], (DEPTH, len(POOL_WINDOWS), POOL_GROUP, POOL_GROUP), f32) * POOL_GROUP ** -0.5
    pool_scale = 1.0 + 0.1 * jax.random.normal(ks[4], (DEPTH, POOL_WIDTH), f32)
    pool_out_norm_g = 1.0 + 0.05 * jax.random.normal(ks[5], (DEPTH, POOL_WIDTH), f32)
    attn_out_norm_g = 1.0 + 0.05 * jax.random.normal(ks[6], (DEPTH, ATTN_WIDTH), f32)
    w_out = jax.random.normal(ks[7], (DEPTH, MIX_WIDTH, D_MODEL), f32) * MIX_WIDTH ** -0.5
    norm_mlp_g = 1.0 + 0.05 * jax.random.normal(ks[8], (DEPTH, D_MODEL), f32)
    w_up = jax.random.normal(ks[9], (DEPTH, D_MODEL, D_FF), f32) * D_MODEL ** -0.5
    w_down = jax.random.normal(ks[10], (DEPTH, D_FF, D_MODEL), f32) * D_FF ** -0.5
    norm_final_g = 1.0 + 0.05 * jax.random.normal(ks[11], (D_MODEL,), f32)
    return {"x": x, "norm_mix_g": norm_mix_g, "w_in": w_in, "pool_w": pool_w,
            "pool_scale": pool_scale, "pool_out_norm_g": pool_out_norm_g,
            "attn_out_norm_g": attn_out_norm_g, "w_out": w_out, "norm_mlp_g": norm_mlp_g,
            "w_up": w_up, "w_down": w_down, "norm_final_g": norm_final_g}


def reference(x, norm_mix_g, w_in, pool_w, pool_scale, pool_out_norm_g, attn_out_norm_g,
              w_out, norm_mlp_g, w_up, w_down, norm_final_g):
    B, S, _ = x.shape
    slopes = alibi_slopes()
    for l in range(DEPTH):
        h = rmsnorm(x, norm_mix_g[l])
        proj = h @ w_in[l]
        u = proj[..., :POOL_WIDTH]
        qkv = proj[..., POOL_WIDTH:].reshape(B, S, 3, N_HEADS, HEAD_DIM)
        y_pool = rmsnorm(pool_mixer(u, pool_w[l], pool_scale[l]), pool_out_norm_g[l])
        y_attn = rmsnorm(dilated_attention(qkv[:, :, 0], qkv[:, :, 1], qkv[:, :, 2], slopes),
                         attn_out_norm_g[l])
        x = x + jnp.concatenate([y_pool, y_attn], axis=-1) @ w_out[l]
        h = rmsnorm(x, norm_mlp_g[l])
        x = x + jnp.square(jax.nn.relu(h @ w_up[l])) @ w_down[l]
    return rmsnorm(x, norm_final_g)
```

```python
import functools

import jax
import jax.numpy as jnp
from jax import lax
from jax.experimental import pallas as pl
from jax.experimental.pallas import tpu as pltpu

F32 = jnp.float32
BF16 = jnp.bfloat16

D_MODEL = 2048
POOL_WIDTH = 1024
POOL_WINDOWS = (2, 4, 8, 16)
POOL_GROUP = 256
MAX_POOL_WINDOW = 16
ATTN_WIDTH = 1024
HEAD_DIM = 128
N_HEADS = 8
DILATIONS = (1, 4, 16)
WINDOW_STEPS = 128
ATTN_TILE = 2048
D_FF = 8192
NORM_EPS = 1e-6

VMEM_LIMIT_BYTES = 60000 * 1024

TM_IN = 256
TM_OUT = 512
TM_MLP = 512
TF_MLP = 1024
MERGE_ROWS = 256


def _rms_scale(v, width):
    return lax.rsqrt(jnp.sum(v * v, axis=-1, keepdims=True) * (1.0 / width) + NORM_EPS)


def _in_proj_body(x_ref, g_ref, w_ref, pw_ref, ps_ref, pg_ref,
                  ypool_ref, q1_ref, k1_ref, v1_ref, q4_ref, k4_ref, v4_ref,
                  q16_ref, k16_ref, v16_ref,
                  ext_ref, nat_ref, m4_ref):
    tm = TM_IN
    i = pl.program_id(1)
    x = x_ref[...]
    h = (x * _rms_scale(x, D_MODEL) * g_ref[...]).astype(BF16)

    u = jnp.dot(h, w_ref[:, 0:POOL_WIDTH], preferred_element_type=F32)

    @pl.when(i == 0)
    def _():
        ext_ref[0:MAX_POOL_WINDOW, :] = jnp.zeros((MAX_POOL_WINDOW, POOL_WIDTH), F32)

    ext_ref[MAX_POOL_WINDOW:MAX_POOL_WINDOW + tm, :] = u
    pos1 = (i * tm + lax.broadcasted_iota(jnp.int32, (tm, 1), 0) + 1).astype(F32)
    ys = []
    for g, w in enumerate(POOL_WINDOWS):
        c0 = g * POOL_GROUP
        cols = slice(c0, c0 + POOL_GROUP)
        tr = ext_ref[MAX_POOL_WINDOW:MAX_POOL_WINDOW + tm, cols]
        for s in range(1, w):
            tr = tr + ext_ref[MAX_POOL_WINDOW - s:MAX_POOL_WINDOW - s + tm, cols]
        inv_cnt = 1.0 / jnp.minimum(pos1, float(w))
        d = tr * inv_cnt - u[:, cols]
        y = jnp.dot(d.astype(BF16), pw_ref[g], preferred_element_type=F32)
        ys.append(y * ps_ref[:, cols])
    ext_ref[0:MAX_POOL_WINDOW, :] = ext_ref[tm:tm + MAX_POOL_WINDOW, :]
    ssq = jnp.sum(ys[0] * ys[0], axis=-1, keepdims=True)
    for y in ys[1:]:
        ssq = ssq + jnp.sum(y * y, axis=-1, keepdims=True)
    r = lax.rsqrt(ssq * (1.0 / POOL_WIDTH) + NORM_EPS)
    for g in range(len(POOL_WINDOWS)):
        cols = slice(g * POOL_GROUP, (g + 1) * POOL_GROUP)
        ypool_ref[:, cols] = (ys[g] * r * pg_ref[:, cols]).astype(BF16)

    scale = HEAD_DIM ** -0.5
    outs = ((q1_ref, q4_ref, q16_ref), (k1_ref, k4_ref, k16_ref), (v1_ref, v4_ref, v16_ref))
    for c, (o1_ref, o4_ref, o16_ref) in enumerate(outs):
        lo = POOL_WIDTH + c * ATTN_WIDTH
        pr = jnp.dot(h, w_ref[:, lo:lo + ATTN_WIDTH], preferred_element_type=F32)
        if c == 0:
            pr = pr * scale
        for hh in range(N_HEADS):
            blk = pr[:, hh * HEAD_DIM:(hh + 1) * HEAD_DIM]
            nat_ref[hh] = blk
            if c != 0:
                o1_ref[hh] = blk.astype(BF16)
        for hh in range(N_HEADS):
            if c == 0:
                for b128 in range(tm // 128):
                    for r4 in range(4):
                        rows = nat_ref[hh, pl.ds(b128 * 128 + r4, 32, stride=4), :]
                        o1_ref[hh, b128 * 128 + r4 * 32:b128 * 128 + (r4 + 1) * 32, :] = rows.astype(BF16)
            for r4 in range(4):
                rows = nat_ref[hh, pl.ds(r4, tm // 4, stride=4), :]
                o4_ref[hh, r4] = rows.astype(BF16)
                m4_ref[hh, r4] = rows
            for r16 in range(16):
                rows = m4_ref[hh, r16 % 4, pl.ds(r16 // 4, tm // 16, stride=4), :]
                o16_ref[hh, r16] = rows.astype(BF16)


def _in_proj(x, g, w_in, pool_w, pool_scale, pool_g):
    B, S, _ = x.shape
    tm = TM_IN
    tiles_per_attn = ATTN_TILE // tm
    n_attn = S // ATTN_TILE
    const = lambda b, i: (0, 0)
    o1_spec = pl.BlockSpec((None, N_HEADS, tm, HEAD_DIM), lambda b, i: (b, 0, i, 0))
    o4_spec = pl.BlockSpec((None, N_HEADS, None, 4, tm // 4, HEAD_DIM),
                           lambda b, i: (b, 0, i // tiles_per_attn, 0, i % tiles_per_attn, 0))
    o16_spec = pl.BlockSpec((None, N_HEADS, None, 16, tm // 16, HEAD_DIM),
                            lambda b, i: (b, 0, i // tiles_per_attn, 0, i % tiles_per_attn, 0))
    s1 = jax.ShapeDtypeStruct((B, N_HEADS, S, HEAD_DIM), BF16)
    s4 = jax.ShapeDtypeStruct((B, N_HEADS, n_attn, 4, ATTN_TILE // 4, HEAD_DIM), BF16)
    s16 = jax.ShapeDtypeStruct((B, N_HEADS, n_attn, 16, ATTN_TILE // 16, HEAD_DIM), BF16)
    return pl.pallas_call(
        _in_proj_body,
        grid=(B, S // tm),
        in_specs=[
            pl.BlockSpec((None, tm, D_MODEL), lambda b, i: (b, i, 0)),
            pl.BlockSpec((1, D_MODEL), const),
            pl.BlockSpec((D_MODEL, POOL_WIDTH + 3 * ATTN_WIDTH), const, pipeline_mode=pl.Buffered(1)),
            pl.BlockSpec((len(POOL_WINDOWS), POOL_GROUP, POOL_GROUP), lambda b, i: (0, 0, 0),
                         pipeline_mode=pl.Buffered(1)),
            pl.BlockSpec((1, POOL_WIDTH), const),
            pl.BlockSpec((1, POOL_WIDTH), const),
        ],
        out_specs=[
            pl.BlockSpec((None, tm, POOL_WIDTH), lambda b, i: (b, i, 0)),
            o1_spec, o1_spec, o1_spec, o4_spec, o4_spec, o4_spec, o16_spec, o16_spec, o16_spec,
        ],
        out_shape=[jax.ShapeDtypeStruct((B, S, POOL_WIDTH), BF16),
                   s1, s1, s1, s4, s4, s4, s16, s16, s16],
        scratch_shapes=[
            pltpu.VMEM((tm + MAX_POOL_WINDOW, POOL_WIDTH), F32),
            pltpu.VMEM((N_HEADS, tm, HEAD_DIM), F32),
            pltpu.VMEM((N_HEADS, 4, tm // 4, HEAD_DIM), F32),
        ],
        compiler_params=pltpu.CompilerParams(
            dimension_semantics=("arbitrary", "arbitrary"), vmem_limit_bytes=VMEM_LIMIT_BYTES),
        name="in_proj",
    )(x, g, w_in, pool_w, pool_scale, pool_g)


KB1_ROWS = WINDOW_STEPS + ATTN_TILE
KB4_ROWS = WINDOW_STEPS + ATTN_TILE // 4
KB16_ROWS = WINDOW_STEPS + ATTN_TILE // 16
UNIT = 128
CLS4 = ATTN_TILE // 4


def _attn_unit(q, kk, vv, bias):
    s = lax.dot_general(q, kk, (((1,), (1,)), ((), ())), preferred_element_type=F32) + bias
    m = jnp.max(s, axis=-1, keepdims=True)
    p = jnp.exp(s - m)
    acc = jnp.dot(p.astype(BF16), vv, preferred_element_type=F32)
    return acc[:, :HEAD_DIM], acc[:, HEAD_DIM:], jnp.broadcast_to(m, (UNIT, HEAD_DIM))


def _attention_body(slopes_ref, q1_ref, k1_ref, v1_ref, q4_ref, k4_ref, v4_ref,
                    q16_ref, k16_ref, v16_ref, o_ref,
                    kb1, vb1, kb4, vb4, kb16, vb16, st, bias_ref, nat_ref):
    h = pl.program_id(1)
    t = pl.program_id(2)
    ninf = -jnp.inf

    @pl.when(t == 0)
    def _():
        kb1[0:UNIT, :] = jnp.zeros((UNIT, HEAD_DIM), BF16)
        kb4[:, 0:UNIT, :] = jnp.zeros((4, UNIT, HEAD_DIM), BF16)
        kb16[:, 0:UNIT, :] = jnp.zeros((16, UNIT, HEAD_DIM), BF16)
        vb1[0:UNIT, 0:HEAD_DIM] = jnp.zeros((UNIT, HEAD_DIM), BF16)
        vb4[:, 0:UNIT, 0:HEAD_DIM] = jnp.zeros((4, UNIT, HEAD_DIM), BF16)
        vb16[:, 0:UNIT, 0:HEAD_DIM] = jnp.zeros((16, UNIT, HEAD_DIM), BF16)
        vb1[:, HEAD_DIM:] = jnp.ones((KB1_ROWS, HEAD_DIM), BF16)
        vb4[:, :, HEAD_DIM:] = jnp.ones((4, KB4_ROWS, HEAD_DIM), BF16)
        vb16[:, :, HEAD_DIM:] = jnp.ones((16, KB16_ROWS, HEAD_DIM), BF16)

    kb1[UNIT:, :] = k1_ref[...]
    vb1[UNIT:, 0:HEAD_DIM] = v1_ref[...]
    kb4[:, UNIT:, :] = k4_ref[...]
    vb4[:, UNIT:, 0:HEAD_DIM] = v4_ref[...]
    kb16[:, UNIT:, :] = k16_ref[...]
    vb16[:, UNIT:, 0:HEAD_DIM] = v16_ref[...]

    slope = slopes_ref[h]
    row = lax.broadcasted_iota(jnp.int32, (UNIT, 2 * UNIT), 0)
    col = lax.broadcasted_iota(jnp.int32, (UNIT, 2 * UNIT), 1)
    row_perm = (row % 32) * 4 + row // 32
    first_pen = jnp.where((col < UNIT) & (t == 0), ninf, 0.0).astype(F32)
    for p, dil in enumerate(DILATIONS):
        j = (row_perm if dil == 1 else row) + UNIT - col
        valid = (j >= 0) & (j <= WINDOW_STEPS)
        b = jnp.where(valid, -(slope * float(dil)) * j.astype(F32), ninf)
        bias_ref[p, 0] = b
        bias_ref[p, 1] = b + first_pen

    def body4(n, carry):
        r4 = n // 4
        a = n % 4
        off = pl.multiple_of(a * UNIT, UNIT)
        q = q4_ref[r4, pl.ds(off, UNIT), :]
        kk = kb4[r4, pl.ds(off, 2 * UNIT), :]
        vv = vb4[r4, pl.ds(off, 2 * UNIT), :]
        num, den, mx = _attn_unit(q, kk, vv, bias_ref[1, (a == 0).astype(jnp.int32)])
        base = pl.multiple_of(r4 * CLS4 + a * UNIT, UNIT)
        st[3, pl.ds(base, UNIT), :] = num
        st[4, pl.ds(base, UNIT), :] = den
        st[5, pl.ds(base, UNIT), :] = mx
        return carry

    lax.fori_loop(0, 16, body4, 0, unroll=2)

    def body1(u, carry):
        off = pl.multiple_of(u * UNIT, UNIT)
        q = q1_ref[pl.ds(off, UNIT), :]
        kk = kb1[pl.ds(off, 2 * UNIT), :]
        vv = vb1[pl.ds(off, 2 * UNIT), :]
        num, den, mx = _attn_unit(q, kk, vv, bias_ref[0, (u == 0).astype(jnp.int32)])
        for r4 in range(4):
            base = pl.multiple_of(r4 * CLS4 + u * 32, 32)
            rows = slice(r4 * 32, (r4 + 1) * 32)
            st[0, pl.ds(base, 32), :] = num[rows]
            st[1, pl.ds(base, 32), :] = den[rows]
            st[2, pl.ds(base, 32), :] = mx[rows]
        return carry

    lax.fori_loop(0, 16, body1, 0, unroll=2)

    def body16(r16, carry):
        q = q16_ref[r16]
        kk = kb16[r16]
        vv = vb16[r16]
        num, den, mx = _attn_unit(q, kk, vv, bias_ref[2, 1])
        base = (r16 % 4) * CLS4 + r16 // 4
        st[6, pl.ds(base, UNIT, stride=4), :] = num
        st[7, pl.ds(base, UNIT, stride=4), :] = den
        st[8, pl.ds(base, UNIT, stride=4), :] = mx
        return carry

    lax.fori_loop(0, 16, body16, 0, unroll=2)

    def merge(c, carry):
        r0 = pl.multiple_of(c * MERGE_ROWS, MERGE_ROWS)
        rows = pl.ds(r0, MERGE_ROWS)
        m1, m4, m16 = st[2, rows, :], st[5, rows, :], st[8, rows, :]
        mm = jnp.maximum(jnp.maximum(m1, m4), m16)
        w1, w4, w16 = jnp.exp(m1 - mm), jnp.exp(m4 - mm), jnp.exp(m16 - mm)
        den = w1 * st[1, rows, :] + w4 * st[4, rows, :] + w16 * st[7, rows, :]
        num = w1 * st[0, rows, :] + w4 * st[3, rows, :] + w16 * st[6, rows, :]
        r4 = r0 // CLS4
        j0 = r0 % CLS4
        nat_ref[pl.ds(4 * j0 + r4, MERGE_ROWS, stride=4), :] = num / den
        return carry

    lax.fori_loop(0, ATTN_TILE // MERGE_ROWS, merge, 0)
    o_ref[...] = nat_ref[...].astype(BF16)

    kb1[0:UNIT, :] = kb1[ATTN_TILE:, :]
    vb1[0:UNIT, 0:HEAD_DIM] = vb1[ATTN_TILE:, 0:HEAD_DIM]
    kb4[:, 0:UNIT, :] = kb4[:, CLS4:, :]
    vb4[:, 0:UNIT, 0:HEAD_DIM] = vb4[:, CLS4:, 0:HEAD_DIM]
    kb16[:, 0:UNIT, :] = kb16[:, UNIT:, :]
    vb16[:, 0:UNIT, 0:HEAD_DIM] = vb16[:, UNIT:, 0:HEAD_DIM]


def _attention(slopes, q1, k1, v1, q4, k4, v4, q16, k16, v16):
    B, _, S, _ = q1.shape
    n_attn = S // ATTN_TILE
    s1_spec = pl.BlockSpec((None, None, ATTN_TILE, HEAD_DIM), lambda b, h, t: (b, h, t, 0))
    s4_spec = pl.BlockSpec((None, None, None, 4, CLS4, HEAD_DIM), lambda b, h, t: (b, h, t, 0, 0, 0))
    s16_spec = pl.BlockSpec((None, None, None, 16, UNIT, HEAD_DIM), lambda b, h, t: (b, h, t, 0, 0, 0))
    return pl.pallas_call(
        _attention_body,
        grid=(B, N_HEADS, n_attn),
        in_specs=[pl.BlockSpec(memory_space=pltpu.SMEM),
                  s1_spec, s1_spec, s1_spec, s4_spec, s4_spec, s4_spec, s16_spec, s16_spec, s16_spec],
        out_specs=pl.BlockSpec((None, ATTN_TILE, HEAD_DIM), lambda b, h, t: (b, t, h)),
        out_shape=jax.ShapeDtypeStruct((B, S, ATTN_WIDTH), BF16),
        scratch_shapes=[
            pltpu.VMEM((KB1_ROWS, HEAD_DIM), BF16),
            pltpu.VMEM((KB1_ROWS, 2 * HEAD_DIM), BF16),
            pltpu.VMEM((4, KB4_ROWS, HEAD_DIM), BF16),
            pltpu.VMEM((4, KB4_ROWS, 2 * HEAD_DIM), BF16),
            pltpu.VMEM((16, KB16_ROWS, HEAD_DIM), BF16),
            pltpu.VMEM((16, KB16_ROWS, 2 * HEAD_DIM), BF16),
            pltpu.VMEM((9, ATTN_TILE, HEAD_DIM), F32),
            pltpu.VMEM((3, 2, UNIT, 2 * UNIT), F32),
            pltpu.VMEM((ATTN_TILE, HEAD_DIM), F32),
        ],
        compiler_params=pltpu.CompilerParams(
            dimension_semantics=("arbitrary", "arbitrary", "arbitrary"),
            vmem_limit_bytes=VMEM_LIMIT_BYTES),
        name="dilated_attention",
    )(slopes, q1, k1, v1, q4, k4, v4, q16, k16, v16)


def _out_proj_body(x_ref, yp_ref, ya_ref, ga_ref, w_ref, gm_ref, x1_ref, h_ref):
    o = ya_ref[...].astype(F32)
    ya = (o * _rms_scale(o, ATTN_WIDTH) * ga_ref[...]).astype(BF16)
    z = jnp.dot(yp_ref[...], w_ref[0:POOL_WIDTH, :], preferred_element_type=F32)
    z = z + jnp.dot(ya, w_ref[POOL_WIDTH:, :], preferred_element_type=F32)
    x1 = x_ref[...] + z
    x1_ref[...] = x1
    h_ref[...] = (x1 * _rms_scale(x1, D_MODEL) * gm_ref[...]).astype(BF16)


def _out_proj(x, y_pool, y_attn, g_attn, w_out, g_mlp):
    B, S, _ = x.shape
    tm = TM_OUT
    const = lambda b, i: (0, 0)
    row = lambda width: pl.BlockSpec((None, tm, width), lambda b, i: (b, i, 0))
    return pl.pallas_call(
        _out_proj_body,
        grid=(B, S // tm),
        in_specs=[row(D_MODEL), row(POOL_WIDTH), row(ATTN_WIDTH),
                  pl.BlockSpec((1, ATTN_WIDTH), const),
                  pl.BlockSpec((POOL_WIDTH + ATTN_WIDTH, D_MODEL), const, pipeline_mode=pl.Buffered(1)),
                  pl.BlockSpec((1, D_MODEL), const)],
        out_specs=[row(D_MODEL), row(D_MODEL)],
        out_shape=[jax.ShapeDtypeStruct((B, S, D_MODEL), F32),
                   jax.ShapeDtypeStruct((B, S, D_MODEL), BF16)],
        compiler_params=pltpu.CompilerParams(
            dimension_semantics=("arbitrary", "arbitrary"), vmem_limit_bytes=VMEM_LIMIT_BYTES),
        name="out_proj",
    )(x, y_pool, y_attn, g_attn, w_out, g_mlp)


def _mlp_body(h_ref, x1_ref, wu_ref, wd_ref, gf_ref, o_ref, acc_ref):
    f = pl.program_id(1)
    a = jnp.dot(h_ref[...], wu_ref[...], preferred_element_type=F32)
    a = jnp.maximum(a, 0.0)
    contrib = jnp.dot((a * a).astype(BF16), wd_ref[...], preferred_element_type=F32)

    @pl.when(f == 0)
    def _():
        acc_ref[...] = contrib

    @pl.when(f != 0)
    def _():
        acc_ref[...] += contrib

    @pl.when(f == pl.num_programs(1) - 1)
    def _():
        x2 = x1_ref[...] + acc_ref[...]
        o_ref[...] = x2 * _rms_scale(x2, D_MODEL) * gf_ref[...]


def _mlp(h, x1, w_up, w_down, g_final):
    M = h.shape[0]
    tm, tf = TM_MLP, TF_MLP
    return pl.pallas_call(
        _mlp_body,
        grid=(M // tm, D_FF // tf),
        in_specs=[pl.BlockSpec((tm, D_MODEL), lambda i, f: (i, 0)),
                  pl.BlockSpec((tm, D_MODEL), lambda i, f: (i, 0)),
                  pl.BlockSpec((D_MODEL, tf), lambda i, f: (0, f)),
                  pl.BlockSpec((tf, D_MODEL), lambda i, f: (f, 0)),
                  pl.BlockSpec((1, D_MODEL), lambda i, f: (0, 0))],
        out_specs=pl.BlockSpec((tm, D_MODEL), lambda i, f: (i, 0)),
        out_shape=jax.ShapeDtypeStruct((M, D_MODEL), F32),
        scratch_shapes=[pltpu.VMEM((tm, D_MODEL), F32)],
        compiler_params=pltpu.CompilerParams(
            dimension_semantics=("arbitrary", "arbitrary"), vmem_limit_bytes=VMEM_LIMIT_BYTES),
        name="mlp",
    )(h, x1, w_up, w_down, g_final)


def kernel(x, norm_mix_g, w_in, pool_w, pool_scale, pool_out_norm_g, attn_out_norm_g,
           w_out, norm_mlp_g, w_up, w_down, norm_final_g):
    B, S, D = x.shape
    assert D == D_MODEL and S % ATTN_TILE == 0 and norm_mix_g.shape[0] == 1
    slopes = 2.0 ** (-8.0 * jnp.arange(1, N_HEADS + 1, dtype=F32) / N_HEADS)
    y_pool, q1, k1, v1, q4, k4, v4, q16, k16, v16 = _in_proj(
        x, norm_mix_g[0][None, :], w_in[0].astype(BF16), pool_w[0].astype(BF16),
        pool_scale[0][None, :], pool_out_norm_g[0][None, :])
    y_attn = _attention(slopes, q1, k1, v1, q4, k4, v4, q16, k16, v16)
    x1, h = _out_proj(x, y_pool, y_attn, attn_out_norm_g[0][None, :], w_out[0].astype(BF16),
                      norm_mlp_g[0][None, :])
    out = _mlp(h.reshape(B * S, D), x1.reshape(B * S, D), w_up[0].astype(BF16),
               w_down[0].astype(BF16), norm_final_g[None, :])
    return out.reshape(B, S, D)
```

```python
import functools

import jax
import jax.numpy as jnp
from jax import lax
from jax.experimental import pallas as pl
from jax.experimental.pallas import tpu as pltpu

F32 = jnp.float32
BF16 = jnp.bfloat16

D_MODEL = 2048
POOL_WIDTH = 1024
POOL_WINDOWS = (2, 4, 8, 16)
POOL_GROUP = 256
MAX_POOL_WINDOW = 16
ATTN_WIDTH = 1024
HEAD_DIM = 128
N_HEADS = 8
DILATIONS = (1, 4, 16)
WINDOW_STEPS = 128
ATTN_TILE = 2048
D_FF = 8192
NORM_EPS = 1e-6

VMEM_LIMIT_BYTES = 60000 * 1024

TM_IN = 256
TM_OUT = 512
TM_MLP = 512
TF_MLP = 1024
TN_MLP = 256
N_UP = D_FF // TF_MLP
N_DOWN = D_MODEL // TN_MLP
MERGE_ROWS = 256


def _rms_scale(v, width):
    return lax.rsqrt(jnp.sum(v * v, axis=-1, keepdims=True) * (1.0 / width) + NORM_EPS)


def _in_proj_body(x_ref, g_ref, w_ref, pw_ref, ps_ref, pg_ref,
                  ypool_ref, q1_ref, k1_ref, v1_ref, q4_ref, k4_ref, v4_ref,
                  q16_ref, k16_ref, v16_ref,
                  ext_ref, nat_ref, m4_ref):
    tm = TM_IN
    i = pl.program_id(1)
    x = x_ref[...]
    h = (x * _rms_scale(x, D_MODEL) * g_ref[...]).astype(BF16)

    u = jnp.dot(h, w_ref[:, 0:POOL_WIDTH], preferred_element_type=F32)

    @pl.when(i == 0)
    def _():
        ext_ref[0:MAX_POOL_WINDOW, :] = jnp.zeros((MAX_POOL_WINDOW, POOL_WIDTH), F32)

    ext_ref[MAX_POOL_WINDOW:MAX_POOL_WINDOW + tm, :] = u
    pos1 = (i * tm + lax.broadcasted_iota(jnp.int32, (tm, 1), 0) + 1).astype(F32)
    ys = []
    for g, w in enumerate(POOL_WINDOWS):
        c0 = g * POOL_GROUP
        cols = slice(c0, c0 + POOL_GROUP)
        tr = ext_ref[MAX_POOL_WINDOW:MAX_POOL_WINDOW + tm, cols]
        for s in range(1, w):
            tr = tr + ext_ref[MAX_POOL_WINDOW - s:MAX_POOL_WINDOW - s + tm, cols]
        inv_cnt = 1.0 / jnp.minimum(pos1, float(w))
        d = tr * inv_cnt - u[:, cols]
        y = jnp.dot(d.astype(BF16), pw_ref[g], preferred_element_type=F32)
        ys.append(y * ps_ref[:, cols])
    ext_ref[0:MAX_POOL_WINDOW, :] = ext_ref[tm:tm + MAX_POOL_WINDOW, :]
    ssq = jnp.sum(ys[0] * ys[0], axis=-1, keepdims=True)
    for y in ys[1:]:
        ssq = ssq + jnp.sum(y * y, axis=-1, keepdims=True)
    r = lax.rsqrt(ssq * (1.0 / POOL_WIDTH) + NORM_EPS)
    for g in range(len(POOL_WINDOWS)):
        cols = slice(g * POOL_GROUP, (g + 1) * POOL_GROUP)
        ypool_ref[:, cols] = (ys[g] * r * pg_ref[:, cols]).astype(BF16)

    scale = HEAD_DIM ** -0.5
    outs = ((q1_ref, q4_ref, q16_ref), (k1_ref, k4_ref, k16_ref), (v1_ref, v4_ref, v16_ref))
    for c, (o1_ref, o4_ref, o16_ref) in enumerate(outs):
        lo = POOL_WIDTH + c * ATTN_WIDTH
        pr = jnp.dot(h, w_ref[:, lo:lo + ATTN_WIDTH], preferred_element_type=F32)
        if c == 0:
            pr = pr * scale
        for hh in range(N_HEADS):
            blk = pr[:, hh * HEAD_DIM:(hh + 1) * HEAD_DIM]
            nat_ref[hh] = blk
            if c != 0:
                o1_ref[hh] = blk.astype(BF16)
        for hh in range(N_HEADS):
            if c == 0:
                for b128 in range(tm // 128):
                    for r4 in range(4):
                        rows = nat_ref[hh, pl.ds(b128 * 128 + r4, 32, stride=4), :]
                        o1_ref[hh, b128 * 128 + r4 * 32:b128 * 128 + (r4 + 1) * 32, :] = rows.astype(BF16)
            for r4 in range(4):
                rows = nat_ref[hh, pl.ds(r4, tm // 4, stride=4), :]
                o4_ref[hh, r4] = rows.astype(BF16)
                m4_ref[hh, r4] = rows
            for r16 in range(16):
                rows = m4_ref[hh, r16 % 4, pl.ds(r16 // 4, tm // 16, stride=4), :]
                o16_ref[hh, r16] = rows.astype(BF16)


def _in_proj(x, g, w_in, pool_w, pool_scale, pool_g):
    B, S, _ = x.shape
    tm = TM_IN
    tiles_per_attn = ATTN_TILE // tm
    n_attn = S // ATTN_TILE
    const = lambda b, i: (0, 0)
    o1_spec = pl.BlockSpec((None, N_HEADS, tm, HEAD_DIM), lambda b, i: (b, 0, i, 0))
    o4_spec = pl.BlockSpec((None, N_HEADS, None, 4, tm // 4, HEAD_DIM),
                           lambda b, i: (b, 0, i // tiles_per_attn, 0, i % tiles_per_attn, 0))
    o16_spec = pl.BlockSpec((None, N_HEADS, None, 16, tm // 16, HEAD_DIM),
                            lambda b, i: (b, 0, i // tiles_per_attn, 0, i % tiles_per_attn, 0))
    s1 = jax.ShapeDtypeStruct((B, N_HEADS, S, HEAD_DIM), BF16)
    s4 = jax.ShapeDtypeStruct((B, N_HEADS, n_attn, 4, ATTN_TILE // 4, HEAD_DIM), BF16)
    s16 = jax.ShapeDtypeStruct((B, N_HEADS, n_attn, 16, ATTN_TILE // 16, HEAD_DIM), BF16)
    return pl.pallas_call(
        _in_proj_body,
        grid=(B, S // tm),
        in_specs=[
            pl.BlockSpec((None, tm, D_MODEL), lambda b, i: (b, i, 0)),
            pl.BlockSpec((1, D_MODEL), const),
            pl.BlockSpec((D_MODEL, POOL_WIDTH + 3 * ATTN_WIDTH), const, pipeline_mode=pl.Buffered(1)),
            pl.BlockSpec((len(POOL_WINDOWS), POOL_GROUP, POOL_GROUP), lambda b, i: (0, 0, 0),
                         pipeline_mode=pl.Buffered(1)),
            pl.BlockSpec((1, POOL_WIDTH), const),
            pl.BlockSpec((1, POOL_WIDTH), const),
        ],
        out_specs=[
            pl.BlockSpec((None, tm, POOL_WIDTH), lambda b, i: (b, i, 0)),
            o1_spec, o1_spec, o1_spec, o4_spec, o4_spec, o4_spec, o16_spec, o16_spec, o16_spec,
        ],
        out_shape=[jax.ShapeDtypeStruct((B, S, POOL_WIDTH), BF16),
                   s1, s1, s1, s4, s4, s4, s16, s16, s16],
        scratch_shapes=[
            pltpu.VMEM((tm + MAX_POOL_WINDOW, POOL_WIDTH), F32),
            pltpu.VMEM((N_HEADS, tm, HEAD_DIM), F32),
            pltpu.VMEM((N_HEADS, 4, tm // 4, HEAD_DIM), F32),
        ],
        compiler_params=pltpu.CompilerParams(
            dimension_semantics=("arbitrary", "arbitrary"), vmem_limit_bytes=VMEM_LIMIT_BYTES),
        name="in_proj",
    )(x, g, w_in, pool_w, pool_scale, pool_g)


KB1_ROWS = WINDOW_STEPS + ATTN_TILE
KB4_ROWS = WINDOW_STEPS + ATTN_TILE // 4
KB16_ROWS = WINDOW_STEPS + ATTN_TILE // 16
UNIT = 128
CLS4 = ATTN_TILE // 4


def _attn_unit(q, kk, vv, bias):
    s = lax.dot_general(q, kk, (((1,), (1,)), ((), ())), preferred_element_type=F32) + bias
    m = jnp.max(s, axis=-1, keepdims=True)
    p = jnp.exp(s - m)
    acc = jnp.dot(p.astype(BF16), vv, preferred_element_type=F32)
    return acc[:, :HEAD_DIM], acc[:, HEAD_DIM:], jnp.broadcast_to(m, (UNIT, HEAD_DIM))


def _attention_body(slopes_ref, q1_ref, k1_ref, v1_ref, q4_ref, k4_ref, v4_ref,
                    q16_ref, k16_ref, v16_ref, o_ref,
                    kb1, vb1, kb4, vb4, kb16, vb16, st, bias_ref, nat_ref):
    h = pl.program_id(1)
    t = pl.program_id(2)
    ninf = -jnp.inf

    @pl.when(t == 0)
    def _():
        kb1[0:UNIT, :] = jnp.zeros((UNIT, HEAD_DIM), BF16)
        kb4[:, 0:UNIT, :] = jnp.zeros((4, UNIT, HEAD_DIM), BF16)
        kb16[:, 0:UNIT, :] = jnp.zeros((16, UNIT, HEAD_DIM), BF16)
        vb1[0:UNIT, 0:HEAD_DIM] = jnp.zeros((UNIT, HEAD_DIM), BF16)
        vb4[:, 0:UNIT, 0:HEAD_DIM] = jnp.zeros((4, UNIT, HEAD_DIM), BF16)
        vb16[:, 0:UNIT, 0:HEAD_DIM] = jnp.zeros((16, UNIT, HEAD_DIM), BF16)
        vb1[:, HEAD_DIM:] = jnp.ones((KB1_ROWS, HEAD_DIM), BF16)
        vb4[:, :, HEAD_DIM:] = jnp.ones((4, KB4_ROWS, HEAD_DIM), BF16)
        vb16[:, :, HEAD_DIM:] = jnp.ones((16, KB16_ROWS, HEAD_DIM), BF16)

    kb1[UNIT:, :] = k1_ref[...]
    vb1[UNIT:, 0:HEAD_DIM] = v1_ref[...]
    kb4[:, UNIT:, :] = k4_ref[...]
    vb4[:, UNIT:, 0:HEAD_DIM] = v4_ref[...]
    kb16[:, UNIT:, :] = k16_ref[...]
    vb16[:, UNIT:, 0:HEAD_DIM] = v16_ref[...]

    slope = slopes_ref[h]
    row = lax.broadcasted_iota(jnp.int32, (UNIT, 2 * UNIT), 0)
    col = lax.broadcasted_iota(jnp.int32, (UNIT, 2 * UNIT), 1)
    row_perm = (row % 32) * 4 + row // 32
    first_pen = jnp.where((col < UNIT) & (t == 0), ninf, 0.0).astype(F32)
    for p, dil in enumerate(DILATIONS):
        j = (row_perm if dil == 1 else row) + UNIT - col
        valid = (j >= 0) & (j <= WINDOW_STEPS)
        b = jnp.where(valid, -(slope * float(dil)) * j.astype(F32), ninf)
        bias_ref[p, 0] = b
        bias_ref[p, 1] = b + first_pen

    for r4 in range(4):
        for a in range(4):
            off = a * UNIT
            num, den, mx = _attn_unit(q4_ref[r4, off:off + UNIT, :], kb4[r4, off:off + 2 * UNIT, :],
                                      vb4[r4, off:off + 2 * UNIT, :], bias_ref[1, int(a == 0)])
            base = r4 * CLS4 + off
            st[3, base:base + UNIT, :] = num
            st[4, base:base + UNIT, :] = den
            st[5, base:base + UNIT, :] = mx

    for u in range(ATTN_TILE // UNIT):
        off = u * UNIT
        num, den, mx = _attn_unit(q1_ref[off:off + UNIT, :], kb1[off:off + 2 * UNIT, :],
                                  vb1[off:off + 2 * UNIT, :], bias_ref[0, int(u == 0)])
        for r4 in range(4):
            base = r4 * CLS4 + u * 32
            rows = slice(r4 * 32, (r4 + 1) * 32)
            st[0, base:base + 32, :] = num[rows]
            st[1, base:base + 32, :] = den[rows]
            st[2, base:base + 32, :] = mx[rows]

    for r16 in range(16):
        num, den, mx = _attn_unit(q16_ref[r16], kb16[r16], vb16[r16], bias_ref[2, 1])
        base = (r16 % 4) * CLS4 + r16 // 4
        st[6, pl.ds(base, UNIT, stride=4), :] = num
        st[7, pl.ds(base, UNIT, stride=4), :] = den
        st[8, pl.ds(base, UNIT, stride=4), :] = mx

    def merge(c, carry):
        r0 = pl.multiple_of(c * MERGE_ROWS, MERGE_ROWS)
        rows = pl.ds(r0, MERGE_ROWS)
        m1, m4, m16 = st[2, rows, :], st[5, rows, :], st[8, rows, :]
        mm = jnp.maximum(jnp.maximum(m1, m4), m16)
        w1, w4, w16 = jnp.exp(m1 - mm), jnp.exp(m4 - mm), jnp.exp(m16 - mm)
        den = w1 * st[1, rows, :] + w4 * st[4, rows, :] + w16 * st[7, rows, :]
        num = w1 * st[0, rows, :] + w4 * st[3, rows, :] + w16 * st[6, rows, :]
        r4 = r0 // CLS4
        j0 = r0 % CLS4
        nat_ref[pl.ds(4 * j0 + r4, MERGE_ROWS, stride=4), :] = num / den
        return carry

    lax.fori_loop(0, ATTN_TILE // MERGE_ROWS, merge, 0)
    o_ref[...] = nat_ref[...].astype(BF16)

    kb1[0:UNIT, :] = kb1[ATTN_TILE:, :]
    vb1[0:UNIT, 0:HEAD_DIM] = vb1[ATTN_TILE:, 0:HEAD_DIM]
    kb4[:, 0:UNIT, :] = kb4[:, CLS4:, :]
    vb4[:, 0:UNIT, 0:HEAD_DIM] = vb4[:, CLS4:, 0:HEAD_DIM]
    kb16[:, 0:UNIT, :] = kb16[:, UNIT:, :]
    vb16[:, 0:UNIT, 0:HEAD_DIM] = vb16[:, UNIT:, 0:HEAD_DIM]


def _attention(slopes, q1, k1, v1, q4, k4, v4, q16, k16, v16):
    B, _, S, _ = q1.shape
    n_attn = S // ATTN_TILE
    s1_spec = pl.BlockSpec((None, None, ATTN_TILE, HEAD_DIM), lambda b, h, t: (b, h, t, 0))
    s4_spec = pl.BlockSpec((None, None, None, 4, CLS4, HEAD_DIM), lambda b, h, t: (b, h, t, 0, 0, 0))
    s16_spec = pl.BlockSpec((None, None, None, 16, UNIT, HEAD_DIM), lambda b, h, t: (b, h, t, 0, 0, 0))
    return pl.pallas_call(
        _attention_body,
        grid=(B, N_HEADS, n_attn),
        in_specs=[pl.BlockSpec(memory_space=pltpu.SMEM),
                  s1_spec, s1_spec, s1_spec, s4_spec, s4_spec, s4_spec, s16_spec, s16_spec, s16_spec],
        out_specs=pl.BlockSpec((None, ATTN_TILE, HEAD_DIM), lambda b, h, t: (b, t, h)),
        out_shape=jax.ShapeDtypeStruct((B, S, ATTN_WIDTH), BF16),
        scratch_shapes=[
            pltpu.VMEM((KB1_ROWS, HEAD_DIM), BF16),
            pltpu.VMEM((KB1_ROWS, 2 * HEAD_DIM), BF16),
            pltpu.VMEM((4, KB4_ROWS, HEAD_DIM), BF16),
            pltpu.VMEM((4, KB4_ROWS, 2 * HEAD_DIM), BF16),
            pltpu.VMEM((16, KB16_ROWS, HEAD_DIM), BF16),
            pltpu.VMEM((16, KB16_ROWS, 2 * HEAD_DIM), BF16),
            pltpu.VMEM((9, ATTN_TILE, HEAD_DIM), F32),
            pltpu.VMEM((3, 2, UNIT, 2 * UNIT), F32),
            pltpu.VMEM((ATTN_TILE, HEAD_DIM), F32),
        ],
        compiler_params=pltpu.CompilerParams(
            dimension_semantics=("arbitrary", "arbitrary", "arbitrary"),
            vmem_limit_bytes=VMEM_LIMIT_BYTES),
        name="dilated_attention",
    )(slopes, q1, k1, v1, q4, k4, v4, q16, k16, v16)


def _out_proj_body(x_ref, yp_ref, ya_ref, ga_ref, w_ref, gm_ref, x1_ref, h_ref):
    o = ya_ref[...].astype(F32)
    ya = (o * _rms_scale(o, ATTN_WIDTH) * ga_ref[...]).astype(BF16)
    z = jnp.dot(yp_ref[...], w_ref[0:POOL_WIDTH, :], preferred_element_type=F32)
    z = z + jnp.dot(ya, w_ref[POOL_WIDTH:, :], preferred_element_type=F32)
    x1 = x_ref[...] + z
    x1_ref[...] = x1
    h_ref[...] = (x1 * _rms_scale(x1, D_MODEL) * gm_ref[...]).astype(BF16)


def _out_proj(x, y_pool, y_attn, g_attn, w_out, g_mlp):
    B, S, _ = x.shape
    tm = TM_OUT
    const = lambda b, i: (0, 0)
    row = lambda width: pl.BlockSpec((None, tm, width), lambda b, i: (b, i, 0))
    return pl.pallas_call(
        _out_proj_body,
        grid=(B, S // tm),
        in_specs=[row(D_MODEL), row(POOL_WIDTH), row(ATTN_WIDTH),
                  pl.BlockSpec((1, ATTN_WIDTH), const),
                  pl.BlockSpec((POOL_WIDTH + ATTN_WIDTH, D_MODEL), const, pipeline_mode=pl.Buffered(1)),
                  pl.BlockSpec((1, D_MODEL), const)],
        out_specs=[row(D_MODEL), row(D_MODEL)],
        out_shape=[jax.ShapeDtypeStruct((B, S, D_MODEL), F32),
                   jax.ShapeDtypeStruct((B, S, D_MODEL), BF16)],
        compiler_params=pltpu.CompilerParams(
            dimension_semantics=("arbitrary", "arbitrary"), vmem_limit_bytes=VMEM_LIMIT_BYTES),
        name="out_proj",
    )(x, y_pool, y_attn, g_attn, w_out, g_mlp)


def _mlp_body(h_ref, x1_ref, wu_ref, wd_ref, gf_ref, o_ref, a_ref, xs_ref):
    s = pl.program_id(1)

    @pl.when(s < N_UP)
    def _():
        a = jnp.maximum(jnp.dot(h_ref[...], wu_ref[...], preferred_element_type=F32), 0.0)
        a_ref[s] = (a * a).astype(BF16)

    @pl.when(s >= N_UP)
    def _():
        y = jnp.dot(a_ref[0], wd_ref[0:TF_MLP, :], preferred_element_type=F32)
        for f in range(1, N_UP):
            y = y + jnp.dot(a_ref[f], wd_ref[f * TF_MLP:(f + 1) * TF_MLP, :], preferred_element_type=F32)
        xs_ref[s - N_UP] = x1_ref[...] + y

    @pl.when(s == N_UP + N_DOWN - 1)
    def _():
        ssq = jnp.sum(xs_ref[0] * xs_ref[0], axis=-1, keepdims=True)
        for n in range(1, N_DOWN):
            ssq = ssq + jnp.sum(xs_ref[n] * xs_ref[n], axis=-1, keepdims=True)
        r = lax.rsqrt(ssq * (1.0 / D_MODEL) + NORM_EPS)
        for n in range(N_DOWN):
            cols = slice(n * TN_MLP, (n + 1) * TN_MLP)
            o_ref[:, cols] = xs_ref[n] * r * gf_ref[:, cols]


def _mlp(h, x1, w_up, w_down, g_final):
    M = h.shape[0]
    tm = TM_MLP
    down_idx = lambda s: jnp.maximum(s - N_UP, 0)
    return pl.pallas_call(
        _mlp_body,
        grid=(M // tm, N_UP + N_DOWN),
        in_specs=[pl.BlockSpec((tm, D_MODEL), lambda i, s: (i, 0)),
                  pl.BlockSpec((tm, TN_MLP), lambda i, s: (i, down_idx(s))),
                  pl.BlockSpec((D_MODEL, TF_MLP), lambda i, s: (0, jnp.minimum(s, N_UP - 1))),
                  pl.BlockSpec((D_FF, TN_MLP), lambda i, s: (0, down_idx(s))),
                  pl.BlockSpec((1, D_MODEL), lambda i, s: (0, 0))],
        out_specs=pl.BlockSpec((tm, D_MODEL), lambda i, s: (i, 0)),
        out_shape=jax.ShapeDtypeStruct((M, D_MODEL), F32),
        scratch_shapes=[pltpu.VMEM((N_UP, tm, TF_MLP), BF16),
                        pltpu.VMEM((N_DOWN, tm, TN_MLP), F32)],
        compiler_params=pltpu.CompilerParams(
            dimension_semantics=("arbitrary", "arbitrary"), vmem_limit_bytes=VMEM_LIMIT_BYTES),
        name="mlp",
    )(h, x1, w_up, w_down, g_final)


def kernel(x, norm_mix_g, w_in, pool_w, pool_scale, pool_out_norm_g, attn_out_norm_g,
           w_out, norm_mlp_g, w_up, w_down, norm_final_g):
    B, S, D = x.shape
    assert D == D_MODEL and S % ATTN_TILE == 0 and norm_mix_g.shape[0] == 1
    slopes = 2.0 ** (-8.0 * jnp.arange(1, N_HEADS + 1, dtype=F32) / N_HEADS)
    y_pool, q1, k1, v1, q4, k4, v4, q16, k16, v16 = _in_proj(
        x, norm_mix_g[0][None, :], w_in[0].astype(BF16), pool_w[0].astype(BF16),
        pool_scale[0][None, :], pool_out_norm_g[0][None, :])
    y_attn = _attention(slopes, q1, k1, v1, q4, k4, v4, q16, k16, v16)
    x1, h = _out_proj(x, y_pool, y_attn, attn_out_norm_g[0][None, :], w_out[0].astype(BF16),
                      norm_mlp_g[0][None, :])
    out = _mlp(h.reshape(B * S, D), x1.reshape(B * S, D), w_up[0].astype(BF16),
               w_down[0].astype(BF16), norm_final_g[None, :])
    return out.reshape(B, S, D)
```

```python
import jax
import jax.numpy as jnp
from jax import lax
from jax.experimental import pallas as pl
from jax.experimental.pallas import tpu as pltpu

F32 = jnp.float32
BF16 = jnp.bfloat16

D_MODEL = 2048
POOL_WIDTH = 1024
POOL_WINDOWS = (2, 4, 8, 16)
POOL_GROUP = 256
MAX_POOL_WINDOW = 16
ATTN_WIDTH = 1024
HEAD_DIM = 128
N_HEADS = 8
DILATIONS = (1, 4, 16)
WINDOW_STEPS = 128
ATTN_TILE = 2048
D_FF = 8192
NORM_EPS = 1e-6
LOG2_E = 1.4426950408889634

VMEM_LIMIT_BYTES = 60000 * 1024

TM_IN = 512
TM_OUT = 512
TM_MLP = 1024
TF_MLP = 512
MERGE_ROWS = 256


def _rms_scale(v, width):
    return lax.rsqrt(jnp.sum(v * v, axis=-1, keepdims=True) * (1.0 / width) + NORM_EPS)


def _in_proj_body(x_ref, g_ref, w_ref, pw_ref, ps_ref, pg_ref,
                  ypool_ref, q1_ref, k1_ref, v1_ref, q4_ref, k4_ref, v4_ref,
                  q16_ref, k16_ref, v16_ref,
                  ext_ref, nat_ref, m4_ref):
    tm = TM_IN
    i = pl.program_id(1)

    @pl.when(i == 0)
    def _():
        ext_ref[0:MAX_POOL_WINDOW, :] = jnp.zeros((MAX_POOL_WINDOW, POOL_WIDTH), F32)

    x = x_ref[...]
    h = (x * _rms_scale(x, D_MODEL) * g_ref[...]).astype(BF16)
    ext_ref[MAX_POOL_WINDOW:, :] = jnp.dot(h, w_ref[:, 0:POOL_WIDTH], preferred_element_type=F32)

    def pool_mixer():
        pos1 = (i * tm + lax.broadcasted_iota(jnp.int32, (tm, 1), 0) + 1).astype(F32)
        ys = []
        for g, w in enumerate(POOL_WINDOWS):
            cols = slice(g * POOL_GROUP, (g + 1) * POOL_GROUP)
            tr = ext_ref[:, cols]
            shift = 1
            while shift < w:
                tr = tr + pltpu.roll(tr, shift, 0)
                shift *= 2
            inv_cnt = 1.0 / jnp.minimum(pos1, float(w))
            d = tr[MAX_POOL_WINDOW:, :] * inv_cnt - ext_ref[MAX_POOL_WINDOW:, cols]
            y = jnp.dot(d.astype(BF16), pw_ref[g], preferred_element_type=F32)
            ys.append(y * ps_ref[:, cols])
        ext_ref[0:MAX_POOL_WINDOW, :] = ext_ref[tm:, :]
        ssq = jnp.sum(ys[0] * ys[0], axis=-1, keepdims=True)
        for y in ys[1:]:
            ssq = ssq + jnp.sum(y * y, axis=-1, keepdims=True)
        r = lax.rsqrt(ssq * (1.0 / POOL_WIDTH) + NORM_EPS)
        for g in range(len(POOL_WINDOWS)):
            cols = slice(g * POOL_GROUP, (g + 1) * POOL_GROUP)
            ypool_ref[:, cols] = (ys[g] * r * pg_ref[:, cols]).astype(BF16)

    def qkv(c, o1_ref, o4_ref, o16_ref):
        lo = POOL_WIDTH + c * ATTN_WIDTH
        pr = jnp.dot(h, w_ref[:, lo:lo + ATTN_WIDTH], preferred_element_type=F32)
        if c == 0:
            pr = pr * (HEAD_DIM ** -0.5 * LOG2_E)
        for hh in range(N_HEADS):
            blk = pr[:, hh * HEAD_DIM:(hh + 1) * HEAD_DIM]
            nat_ref[hh] = blk
            if c != 0:
                o1_ref[hh] = blk.astype(BF16)
        for hh in range(N_HEADS):
            if c == 0:
                for b128 in range(tm // 128):
                    for r4 in range(4):
                        rows = nat_ref[hh, pl.ds(b128 * 128 + r4, 32, stride=4), :]
                        o1_ref[hh, b128 * 128 + r4 * 32:b128 * 128 + (r4 + 1) * 32, :] = rows.astype(BF16)
            for r4 in range(4):
                rows = nat_ref[hh, pl.ds(r4, tm // 4, stride=4), :]
                o4_ref[hh, r4] = rows.astype(BF16)
                m4_ref[hh, r4] = rows
            for r16 in range(16):
                rows = m4_ref[hh, r16 % 4, pl.ds(r16 // 4, tm // 16, stride=4), :]
                o16_ref[hh, r16] = rows.astype(BF16)

    qkv(0, q1_ref, q4_ref, q16_ref)
    qkv(1, k1_ref, k4_ref, k16_ref)
    pool_mixer()
    qkv(2, v1_ref, v4_ref, v16_ref)


def _in_proj(x, g, w_in, pool_w, pool_scale, pool_g):
    B, S, _ = x.shape
    tm = TM_IN
    tiles_per_attn = ATTN_TILE // tm
    n_attn = S // ATTN_TILE
    const = lambda b, i: (0, 0)
    o1_spec = pl.BlockSpec((None, N_HEADS, tm, HEAD_DIM), lambda b, i: (b, 0, i, 0))
    o4_spec = pl.BlockSpec((None, N_HEADS, None, 4, tm // 4, HEAD_DIM),
                           lambda b, i: (b, 0, i // tiles_per_attn, 0, i % tiles_per_attn, 0))
    o16_spec = pl.BlockSpec((None, N_HEADS, None, 16, tm // 16, HEAD_DIM),
                            lambda b, i: (b, 0, i // tiles_per_attn, 0, i % tiles_per_attn, 0))
    s1 = jax.ShapeDtypeStruct((B, N_HEADS, S, HEAD_DIM), BF16)
    s4 = jax.ShapeDtypeStruct((B, N_HEADS, n_attn, 4, ATTN_TILE // 4, HEAD_DIM), BF16)
    s16 = jax.ShapeDtypeStruct((B, N_HEADS, n_attn, 16, ATTN_TILE // 16, HEAD_DIM), BF16)
    return pl.pallas_call(
        _in_proj_body,
        grid=(B, S // tm),
        in_specs=[
            pl.BlockSpec((None, tm, D_MODEL), lambda b, i: (b, i, 0)),
            pl.BlockSpec((1, D_MODEL), const),
            pl.BlockSpec((D_MODEL, POOL_WIDTH + 3 * ATTN_WIDTH), const, pipeline_mode=pl.Buffered(1)),
            pl.BlockSpec((len(POOL_WINDOWS), POOL_GROUP, POOL_GROUP), lambda b, i: (0, 0, 0),
                         pipeline_mode=pl.Buffered(1)),
            pl.BlockSpec((1, POOL_WIDTH), const),
            pl.BlockSpec((1, POOL_WIDTH), const),
        ],
        out_specs=[
            pl.BlockSpec((None, tm, POOL_WIDTH), lambda b, i: (b, i, 0)),
            o1_spec, o1_spec, o1_spec, o4_spec, o4_spec, o4_spec, o16_spec, o16_spec, o16_spec,
        ],
        out_shape=[jax.ShapeDtypeStruct((B, S, POOL_WIDTH), BF16),
                   s1, s1, s1, s4, s4, s4, s16, s16, s16],
        scratch_shapes=[
            pltpu.VMEM((tm + MAX_POOL_WINDOW, POOL_WIDTH), F32),
            pltpu.VMEM((N_HEADS, tm, HEAD_DIM), F32),
            pltpu.VMEM((N_HEADS, 4, tm // 4, HEAD_DIM), F32),
        ],
        compiler_params=pltpu.CompilerParams(
            dimension_semantics=("arbitrary", "arbitrary"), vmem_limit_bytes=VMEM_LIMIT_BYTES),
        name="in_proj",
    )(x, g, w_in, pool_w, pool_scale, pool_g)


KB1_ROWS = WINDOW_STEPS + ATTN_TILE
KB4_ROWS = WINDOW_STEPS + ATTN_TILE // 4
KB16_ROWS = WINDOW_STEPS + ATTN_TILE // 16
UNIT = 128
CLS4 = ATTN_TILE // 4


def _attn_unit(q, kk, vv, bias):
    s = lax.dot_general(q, kk, (((1,), (1,)), ((), ())), preferred_element_type=F32) + bias
    m = jnp.max(s, axis=-1, keepdims=True)
    p = jnp.exp2(s - m)
    acc = jnp.dot(p.astype(BF16), vv, preferred_element_type=F32)
    return acc[:, :HEAD_DIM], acc[:, HEAD_DIM:], jnp.broadcast_to(m, (UNIT, HEAD_DIM))


def _attention_body(slopes_ref, q1_ref, k1_ref, v1_ref, q4_ref, k4_ref, v4_ref,
                    q16_ref, k16_ref, v16_ref, o_ref,
                    kb1, vb1, kb4, vb4, kb16, vb16, st, bias_ref, nat_ref):
    h = pl.program_id(1)
    t = pl.program_id(2)
    ninf = -jnp.inf

    @pl.when(t == 0)
    def _():
        kb1[0:UNIT, :] = jnp.zeros((UNIT, HEAD_DIM), BF16)
        kb4[:, 0:UNIT, :] = jnp.zeros((4, UNIT, HEAD_DIM), BF16)
        kb16[:, 0:UNIT, :] = jnp.zeros((16, UNIT, HEAD_DIM), BF16)
        vb1[0:UNIT, 0:HEAD_DIM] = jnp.zeros((UNIT, HEAD_DIM), BF16)
        vb4[:, 0:UNIT, 0:HEAD_DIM] = jnp.zeros((4, UNIT, HEAD_DIM), BF16)
        vb16[:, 0:UNIT, 0:HEAD_DIM] = jnp.zeros((16, UNIT, HEAD_DIM), BF16)
        vb1[:, HEAD_DIM:] = jnp.ones((KB1_ROWS, HEAD_DIM), BF16)
        vb4[:, :, HEAD_DIM:] = jnp.ones((4, KB4_ROWS, HEAD_DIM), BF16)
        vb16[:, :, HEAD_DIM:] = jnp.ones((16, KB16_ROWS, HEAD_DIM), BF16)

    @pl.when(t > 0)
    def _():
        kb1[0:UNIT, :] = kb1[ATTN_TILE:, :]
        vb1[0:UNIT, 0:HEAD_DIM] = vb1[ATTN_TILE:, 0:HEAD_DIM]
        kb4[:, 0:UNIT, :] = kb4[:, CLS4:, :]
        vb4[:, 0:UNIT, 0:HEAD_DIM] = vb4[:, CLS4:, 0:HEAD_DIM]
        kb16[:, 0:UNIT, :] = kb16[:, UNIT:, :]
        vb16[:, 0:UNIT, 0:HEAD_DIM] = vb16[:, UNIT:, 0:HEAD_DIM]

    kb1[UNIT:, :] = k1_ref[...]
    vb1[UNIT:, 0:HEAD_DIM] = v1_ref[...]
    kb4[:, UNIT:, :] = k4_ref[...]
    vb4[:, UNIT:, 0:HEAD_DIM] = v4_ref[...]
    kb16[:, UNIT:, :] = k16_ref[...]
    vb16[:, UNIT:, 0:HEAD_DIM] = v16_ref[...]

    slope = slopes_ref[h]
    row = lax.broadcasted_iota(jnp.int32, (UNIT, 2 * UNIT), 0)
    col = lax.broadcasted_iota(jnp.int32, (UNIT, 2 * UNIT), 1)
    row_perm = (row % 32) * 4 + row // 32
    first_pen = jnp.where((col < UNIT) & (t == 0), ninf, 0.0).astype(F32)
    for p, dil in enumerate(DILATIONS):
        j = (row_perm if dil == 1 else row) + UNIT - col
        valid = (j >= 0) & (j <= WINDOW_STEPS)
        b = jnp.where(valid, -(slope * (float(dil) * LOG2_E)) * j.astype(F32), ninf)
        bias_ref[p, 0] = b
        bias_ref[p, 1] = b + first_pen

    def unit4(r4, a):
        off = a * UNIT
        num, den, mx = _attn_unit(q4_ref[r4, off:off + UNIT, :], kb4[r4, off:off + 2 * UNIT, :],
                                  vb4[r4, off:off + 2 * UNIT, :], bias_ref[1, int(a == 0)])
        base = r4 * CLS4 + off
        st[3, base:base + UNIT, :] = num
        st[4, base:base + UNIT, :] = den
        st[5, base:base + UNIT, :] = mx

    def unit1(u):
        off = u * UNIT
        num, den, mx = _attn_unit(q1_ref[off:off + UNIT, :], kb1[off:off + 2 * UNIT, :],
                                  vb1[off:off + 2 * UNIT, :], bias_ref[0, int(u == 0)])
        for r4 in range(4):
            base = r4 * CLS4 + u * 32
            rows = slice(r4 * 32, (r4 + 1) * 32)
            st[0, base:base + 32, :] = num[rows]
            st[1, base:base + 32, :] = den[rows]
            st[2, base:base + 32, :] = mx[rows]

    def unit16(r16):
        num, den, mx = _attn_unit(q16_ref[r16], kb16[r16], vb16[r16], bias_ref[2, 1])
        base = (r16 % 4) * CLS4 + r16 // 4
        st[6, pl.ds(base, UNIT, stride=4), :] = num
        st[7, pl.ds(base, UNIT, stride=4), :] = den
        st[8, pl.ds(base, UNIT, stride=4), :] = mx

    def merge(r4, half):
        j0 = half * MERGE_ROWS
        rows = slice(r4 * CLS4 + j0, r4 * CLS4 + j0 + MERGE_ROWS)
        m1, m4, m16 = st[2, rows, :], st[5, rows, :], st[8, rows, :]
        mm = jnp.maximum(jnp.maximum(m1, m4), m16)
        w1, w4, w16 = jnp.exp2(m1 - mm), jnp.exp2(m4 - mm), jnp.exp2(m16 - mm)
        den = w1 * st[1, rows, :] + w4 * st[4, rows, :] + w16 * st[7, rows, :]
        num = w1 * st[0, rows, :] + w4 * st[3, rows, :] + w16 * st[6, rows, :]
        nat_ref[pl.ds(4 * j0 + r4, MERGE_ROWS, stride=4), :] = num / den

    n_halves = CLS4 // MERGE_ROWS
    units1_per_half = ATTN_TILE // UNIT // n_halves
    for half in range(n_halves):
        for u in range(half * units1_per_half, (half + 1) * units1_per_half):
            unit1(u)
        for r4 in range(4):
            if half == 0:
                for c in range(4):
                    unit16(r4 + 4 * c)
            unit4(r4, 2 * half)
            unit4(r4, 2 * half + 1)
            merge(r4, half)
    o_ref[...] = nat_ref[...].astype(BF16)


def _attention(slopes, q1, k1, v1, q4, k4, v4, q16, k16, v16):
    B, _, S, _ = q1.shape
    n_attn = S // ATTN_TILE
    s1_spec = pl.BlockSpec((None, None, ATTN_TILE, HEAD_DIM), lambda b, h, t: (b, h, t, 0))
    s4_spec = pl.BlockSpec((None, None, None, 4, CLS4, HEAD_DIM), lambda b, h, t: (b, h, t, 0, 0, 0))
    s16_spec = pl.BlockSpec((None, None, None, 16, UNIT, HEAD_DIM), lambda b, h, t: (b, h, t, 0, 0, 0))
    return pl.pallas_call(
        _attention_body,
        grid=(B, N_HEADS, n_attn),
        in_specs=[pl.BlockSpec(memory_space=pltpu.SMEM),
                  s1_spec, s1_spec, s1_spec, s4_spec, s4_spec, s4_spec, s16_spec, s16_spec, s16_spec],
        out_specs=pl.BlockSpec((None, ATTN_TILE, HEAD_DIM), lambda b, h, t: (b, t, h)),
        out_shape=jax.ShapeDtypeStruct((B, S, ATTN_WIDTH), BF16),
        scratch_shapes=[
            pltpu.VMEM((KB1_ROWS, HEAD_DIM), BF16),
            pltpu.VMEM((KB1_ROWS, 2 * HEAD_DIM), BF16),
            pltpu.VMEM((4, KB4_ROWS, HEAD_DIM), BF16),
            pltpu.VMEM((4, KB4_ROWS, 2 * HEAD_DIM), BF16),
            pltpu.VMEM((16, KB16_ROWS, HEAD_DIM), BF16),
            pltpu.VMEM((16, KB16_ROWS, 2 * HEAD_DIM), BF16),
            pltpu.VMEM((9, ATTN_TILE, HEAD_DIM), F32),
            pltpu.VMEM((3, 2, UNIT, 2 * UNIT), F32),
            pltpu.VMEM((ATTN_TILE, HEAD_DIM), F32),
        ],
        compiler_params=pltpu.CompilerParams(
            dimension_semantics=("arbitrary", "arbitrary", "arbitrary"),
            vmem_limit_bytes=VMEM_LIMIT_BYTES),
        name="dilated_attention",
    )(slopes, q1, k1, v1, q4, k4, v4, q16, k16, v16)


def _out_proj_body(x_ref, yp_ref, ya_ref, ga_ref, w_ref, gm_ref, x1_ref, h_ref):
    o = ya_ref[...].astype(F32)
    ya = (o * _rms_scale(o, ATTN_WIDTH) * ga_ref[...]).astype(BF16)
    z = jnp.dot(yp_ref[...], w_ref[0:POOL_WIDTH, :], preferred_element_type=F32)
    z = z + jnp.dot(ya, w_ref[POOL_WIDTH:, :], preferred_element_type=F32)
    x1 = x_ref[...] + z
    x1_ref[...] = x1
    h_ref[...] = (x1 * _rms_scale(x1, D_MODEL) * gm_ref[...]).astype(BF16)


def _out_proj(x, y_pool, y_attn, g_attn, w_out, g_mlp):
    B, S, _ = x.shape
    tm = TM_OUT
    const = lambda b, i: (0, 0)
    row = lambda width: pl.BlockSpec((None, tm, width), lambda b, i: (b, i, 0))
    return pl.pallas_call(
        _out_proj_body,
        grid=(B, S // tm),
        in_specs=[row(D_MODEL), row(POOL_WIDTH), row(ATTN_WIDTH),
                  pl.BlockSpec((1, ATTN_WIDTH), const),
                  pl.BlockSpec((POOL_WIDTH + ATTN_WIDTH, D_MODEL), const, pipeline_mode=pl.Buffered(1)),
                  pl.BlockSpec((1, D_MODEL), const)],
        out_specs=[row(D_MODEL), row(D_MODEL)],
        out_shape=[jax.ShapeDtypeStruct((B, S, D_MODEL), F32),
                   jax.ShapeDtypeStruct((B, S, D_MODEL), BF16)],
        compiler_params=pltpu.CompilerParams(
            dimension_semantics=("arbitrary", "arbitrary"), vmem_limit_bytes=VMEM_LIMIT_BYTES),
        name="out_proj",
    )(x, y_pool, y_attn, g_attn, w_out, g_mlp)


def _mlp_body(h_ref, x1_ref, wu_ref, wd_ref, gf_ref, o_ref):
    f = pl.program_id(1)

    @pl.when(f == 0)
    def _():
        o_ref[...] = x1_ref[...]

    a = jnp.maximum(jnp.dot(h_ref[...], wu_ref[...], preferred_element_type=F32), 0.0)
    o_ref[...] += jnp.dot((a * a).astype(BF16), wd_ref[...], preferred_element_type=F32)

    @pl.when(f == pl.num_programs(1) - 1)
    def _():
        x2 = o_ref[...]
        o_ref[...] = x2 * _rms_scale(x2, D_MODEL) * gf_ref[...]


def _mlp(h, x1, w_up, w_down, g_final):
    M = h.shape[0]
    tm, tf = TM_MLP, TF_MLP
    return pl.pallas_call(
        _mlp_body,
        grid=(M // tm, D_FF // tf),
        in_specs=[pl.BlockSpec((tm, D_MODEL), lambda i, f: (i, 0)),
                  pl.BlockSpec((tm, D_MODEL), lambda i, f: (i, 0)),
                  pl.BlockSpec((D_MODEL, tf), lambda i, f: (0, f)),
                  pl.BlockSpec((tf, D_MODEL), lambda i, f: (f, 0)),
                  pl.BlockSpec((1, D_MODEL), lambda i, f: (0, 0))],
        out_specs=pl.BlockSpec((tm, D_MODEL), lambda i, f: (i, 0)),
        out_shape=jax.ShapeDtypeStruct((M, D_MODEL), F32),
        compiler_params=pltpu.CompilerParams(
            dimension_semantics=("arbitrary", "arbitrary"), vmem_limit_bytes=VMEM_LIMIT_BYTES),
        name="mlp",
    )(h, x1, w_up, w_down, g_final)


def kernel(x, norm_mix_g, w_in, pool_w, pool_scale, pool_out_norm_g, attn_out_norm_g,
           w_out, norm_mlp_g, w_up, w_down, norm_final_g):
    B, S, D = x.shape
    assert D == D_MODEL and S % ATTN_TILE == 0 and norm_mix_g.shape[0] == 1
    slopes = 2.0 ** (-8.0 * jnp.arange(1, N_HEADS + 1, dtype=F32) / N_HEADS)
    y_pool, q1, k1, v1, q4, k4, v4, q16, k16, v16 = _in_proj(
        x, norm_mix_g[0][None, :], w_in[0].astype(BF16), pool_w[0].astype(BF16),
        pool_scale[0][None, :], pool_out_norm_g[0][None, :])
    y_attn = _attention(slopes, q1, k1, v1, q4, k4, v4, q16, k16, v16)
    x1, h = _out_proj(x, y_pool, y_attn, attn_out_norm_g[0][None, :], w_out[0].astype(BF16),
                      norm_mlp_g[0][None, :])
    out = _mlp(h.reshape(B * S, D), x1.reshape(B * S, D), w_up[0].astype(BF16),
               w_down[0].astype(BF16), norm_final_g[None, :])
    return out.reshape(B, S, D)
```

```python
import jax
import jax.numpy as jnp
from jax import lax
from jax.experimental import pallas as pl
from jax.experimental.pallas import tpu as pltpu

F32 = jnp.float32
BF16 = jnp.bfloat16

D_MODEL = 2048
POOL_WIDTH = 1024
POOL_WINDOWS = (2, 4, 8, 16)
POOL_GROUP = 256
MAX_POOL_WINDOW = 16
ATTN_WIDTH = 1024
HEAD_DIM = 128
N_HEADS = 8
DILATIONS = (1, 4, 16)
WINDOW_STEPS = 128
ATTN_TILE = 2048
D_FF = 8192
NORM_EPS = 1e-6
LOG2_E = 1.4426950408889634

VMEM_LIMIT_BYTES = 63 * 1024 * 1024

TM_IN = 512
TM_OUT = 512
TM_MLP = 1024
TF_MLP = 1024
MERGE_ROWS = 256


def _rms_scale(v, width):
    return lax.rsqrt(jnp.sum(v * v, axis=-1, keepdims=True) * (1.0 / width) + NORM_EPS)


def _in_proj_body(x_ref, g_ref, w_ref, pw_ref, ps_ref, pg_ref,
                  ypool_ref, q1_ref, k1_ref, v1_ref, q4_ref, k4_ref, v4_ref,
                  q16_ref, k16_ref, v16_ref,
                  ext_ref, nat_ref, m4_ref):
    tm = TM_IN
    i = pl.program_id(1)

    @pl.when(i == 0)
    def _():
        ext_ref[0:MAX_POOL_WINDOW, :] = jnp.zeros((MAX_POOL_WINDOW, POOL_WIDTH), F32)

    halves = []
    for r0 in range(0, tm, tm // 2):
        xr = x_ref[r0:r0 + tm // 2, :]
        hr = (xr * _rms_scale(xr, D_MODEL) * g_ref[...]).astype(BF16)
        ext_ref[MAX_POOL_WINDOW + r0:MAX_POOL_WINDOW + r0 + tm // 2, :] = jnp.dot(
            hr, w_ref[:, 0:POOL_WIDTH], preferred_element_type=F32)
        halves.append(hr)
    h = jnp.concatenate(halves, axis=0)

    def pool_mixer():
        pos1 = (i * tm + lax.broadcasted_iota(jnp.int32, (tm, 1), 0) + 1).astype(F32)
        ys = []
        for g, w in enumerate(POOL_WINDOWS):
            cols = slice(g * POOL_GROUP, (g + 1) * POOL_GROUP)
            tr = ext_ref[:, cols]
            shift = 1
            while shift < w:
                tr = tr + pltpu.roll(tr, shift, 0)
                shift *= 2
            inv_cnt = 1.0 / jnp.minimum(pos1, float(w))
            d = tr[MAX_POOL_WINDOW:, :] * inv_cnt - ext_ref[MAX_POOL_WINDOW:, cols]
            y = jnp.dot(d.astype(BF16), pw_ref[g], preferred_element_type=F32)
            ys.append(y * ps_ref[:, cols])
        ext_ref[0:MAX_POOL_WINDOW, :] = ext_ref[tm:, :]
        ssq = jnp.sum(ys[0] * ys[0], axis=-1, keepdims=True)
        for y in ys[1:]:
            ssq = ssq + jnp.sum(y * y, axis=-1, keepdims=True)
        r = lax.rsqrt(ssq * (1.0 / POOL_WIDTH) + NORM_EPS)
        for g in range(len(POOL_WINDOWS)):
            cols = slice(g * POOL_GROUP, (g + 1) * POOL_GROUP)
            ypool_ref[:, cols] = (ys[g] * r * pg_ref[:, cols]).astype(BF16)

    def qkv(c, o1_ref, o4_ref, o16_ref):
        lo = POOL_WIDTH + c * ATTN_WIDTH
        pr = jnp.dot(h, w_ref[:, lo:lo + ATTN_WIDTH], preferred_element_type=F32)
        if c == 0:
            pr = pr * (HEAD_DIM ** -0.5 * LOG2_E)
        for hh in range(N_HEADS):
            blk = pr[:, hh * HEAD_DIM:(hh + 1) * HEAD_DIM]
            nat_ref[hh] = blk
            if c != 0:
                o1_ref[hh] = blk.astype(BF16)
        for hh in range(N_HEADS):
            if c == 0:
                for b128 in range(tm // 128):
                    for r4 in range(4):
                        rows = nat_ref[hh, pl.ds(b128 * 128 + r4, 32, stride=4), :]
                        o1_ref[hh, b128 * 128 + r4 * 32:b128 * 128 + (r4 + 1) * 32, :] = rows.astype(BF16)
            for r4 in range(4):
                rows = nat_ref[hh, pl.ds(r4, tm // 4, stride=4), :]
                o4_ref[hh, r4] = rows.astype(BF16)
                m4_ref[hh, r4] = rows
            for r16 in range(16):
                rows = m4_ref[hh, r16 % 4, pl.ds(r16 // 4, tm // 16, stride=4), :]
                o16_ref[hh, r16] = rows.astype(BF16)

    qkv(0, q1_ref, q4_ref, q16_ref)
    qkv(1, k1_ref, k4_ref, k16_ref)
    pool_mixer()
    qkv(2, v1_ref, v4_ref, v16_ref)


def _in_proj(x, g, w_in, pool_w, pool_scale, pool_g):
    B, S, _ = x.shape
    tm = TM_IN
    tiles_per_attn = ATTN_TILE // tm
    n_attn = S // ATTN_TILE
    const = lambda b, i: (0, 0)
    o1_spec = pl.BlockSpec((None, N_HEADS, tm, HEAD_DIM), lambda b, i: (b, 0, i, 0))
    o4_spec = pl.BlockSpec((None, N_HEADS, None, 4, tm // 4, HEAD_DIM),
                           lambda b, i: (b, 0, i // tiles_per_attn, 0, i % tiles_per_attn, 0))
    o16_spec = pl.BlockSpec((None, N_HEADS, None, 16, tm // 16, HEAD_DIM),
                            lambda b, i: (b, 0, i // tiles_per_attn, 0, i % tiles_per_attn, 0))
    s1 = jax.ShapeDtypeStruct((B, N_HEADS, S, HEAD_DIM), BF16)
    s4 = jax.ShapeDtypeStruct((B, N_HEADS, n_attn, 4, ATTN_TILE // 4, HEAD_DIM), BF16)
    s16 = jax.ShapeDtypeStruct((B, N_HEADS, n_attn, 16, ATTN_TILE // 16, HEAD_DIM), BF16)
    return pl.pallas_call(
        _in_proj_body,
        grid=(B, S // tm),
        in_specs=[
            pl.BlockSpec((None, tm, D_MODEL), lambda b, i: (b, i, 0)),
            pl.BlockSpec((1, D_MODEL), const),
            pl.BlockSpec((D_MODEL, POOL_WIDTH + 3 * ATTN_WIDTH), const, pipeline_mode=pl.Buffered(1)),
            pl.BlockSpec((len(POOL_WINDOWS), POOL_GROUP, POOL_GROUP), lambda b, i: (0, 0, 0),
                         pipeline_mode=pl.Buffered(1)),
            pl.BlockSpec((1, POOL_WIDTH), const),
            pl.BlockSpec((1, POOL_WIDTH), const),
        ],
        out_specs=[
            pl.BlockSpec((None, tm, POOL_WIDTH), lambda b, i: (b, i, 0)),
            o1_spec, o1_spec, o1_spec, o4_spec, o4_spec, o4_spec, o16_spec, o16_spec, o16_spec,
        ],
        out_shape=[jax.ShapeDtypeStruct((B, S, POOL_WIDTH), BF16),
                   s1, s1, s1, s4, s4, s4, s16, s16, s16],
        scratch_shapes=[
            pltpu.VMEM((tm + MAX_POOL_WINDOW, POOL_WIDTH), F32),
            pltpu.VMEM((N_HEADS, tm, HEAD_DIM), F32),
            pltpu.VMEM((N_HEADS, 4, tm // 4, HEAD_DIM), F32),
        ],
        compiler_params=pltpu.CompilerParams(
            dimension_semantics=("arbitrary", "arbitrary"), vmem_limit_bytes=VMEM_LIMIT_BYTES),
        name="in_proj",
    )(x, g, w_in, pool_w, pool_scale, pool_g)


KB1_ROWS = WINDOW_STEPS + ATTN_TILE
KB4_ROWS = WINDOW_STEPS + ATTN_TILE // 4
KB16_ROWS = WINDOW_STEPS + ATTN_TILE // 16
UNIT = 128
CLS4 = ATTN_TILE // 4


def _attn_unit(q, kk, vv, bias):
    s = lax.dot_general(q, kk, (((1,), (1,)), ((), ())), preferred_element_type=F32) + bias
    m = jnp.max(s, axis=-1, keepdims=True)
    p = jnp.exp2(s - m)
    acc = jnp.dot(p.astype(BF16), vv, preferred_element_type=F32)
    return acc[:, :HEAD_DIM], acc[:, HEAD_DIM:], jnp.broadcast_to(m, (UNIT, HEAD_DIM))


def _attention_body(slopes_ref, q1_ref, k1_ref, v1_ref, q4_ref, k4_ref, v4_ref,
                    q16_ref, k16_ref, v16_ref, o_ref,
                    kb1, vb1, kb4, vb4, kb16, vb16, st, bias_ref, nat_ref):
    h = pl.program_id(1)
    t = pl.program_id(2)
    ninf = -jnp.inf

    @pl.when(t == 0)
    def _():
        kb1[0:UNIT, :] = jnp.zeros((UNIT, HEAD_DIM), BF16)
        kb4[:, 0:UNIT, :] = jnp.zeros((4, UNIT, HEAD_DIM), BF16)
        kb16[:, 0:UNIT, :] = jnp.zeros((16, UNIT, HEAD_DIM), BF16)
        vb1[0:UNIT, 0:HEAD_DIM] = jnp.zeros((UNIT, HEAD_DIM), BF16)
        vb4[:, 0:UNIT, 0:HEAD_DIM] = jnp.zeros((4, UNIT, HEAD_DIM), BF16)
        vb16[:, 0:UNIT, 0:HEAD_DIM] = jnp.zeros((16, UNIT, HEAD_DIM), BF16)
        vb1[:, HEAD_DIM:] = jnp.ones((KB1_ROWS, HEAD_DIM), BF16)
        vb4[:, :, HEAD_DIM:] = jnp.ones((4, KB4_ROWS, HEAD_DIM), BF16)
        vb16[:, :, HEAD_DIM:] = jnp.ones((16, KB16_ROWS, HEAD_DIM), BF16)

    @pl.when(t > 0)
    def _():
        kb1[0:UNIT, :] = kb1[ATTN_TILE:, :]
        vb1[0:UNIT, 0:HEAD_DIM] = vb1[ATTN_TILE:, 0:HEAD_DIM]
        kb4[:, 0:UNIT, :] = kb4[:, CLS4:, :]
        vb4[:, 0:UNIT, 0:HEAD_DIM] = vb4[:, CLS4:, 0:HEAD_DIM]
        kb16[:, 0:UNIT, :] = kb16[:, UNIT:, :]
        vb16[:, 0:UNIT, 0:HEAD_DIM] = vb16[:, UNIT:, 0:HEAD_DIM]

    kb1[UNIT:, :] = k1_ref[...]
    vb1[UNIT:, 0:HEAD_DIM] = v1_ref[...]
    kb4[:, UNIT:, :] = k4_ref[...]
    vb4[:, UNIT:, 0:HEAD_DIM] = v4_ref[...]
    kb16[:, UNIT:, :] = k16_ref[...]
    vb16[:, UNIT:, 0:HEAD_DIM] = v16_ref[...]

    slope = slopes_ref[h]
    row = lax.broadcasted_iota(jnp.int32, (UNIT, 2 * UNIT), 0)
    col = lax.broadcasted_iota(jnp.int32, (UNIT, 2 * UNIT), 1)
    row_perm = (row % 32) * 4 + row // 32
    first_pen = jnp.where((col < UNIT) & (t == 0), ninf, 0.0).astype(F32)
    for p, dil in enumerate(DILATIONS):
        j = (row_perm if dil == 1 else row) + UNIT - col
        valid = (j >= 0) & (j <= WINDOW_STEPS)
        b = jnp.where(valid, -(slope * (float(dil) * LOG2_E)) * j.astype(F32), ninf)
        bias_ref[p, 0] = b
        bias_ref[p, 1] = b + first_pen

    def unit4(r4, a):
        off = a * UNIT
        num, den, mx = _attn_unit(q4_ref[r4, off:off + UNIT, :], kb4[r4, off:off + 2 * UNIT, :],
                                  vb4[r4, off:off + 2 * UNIT, :], bias_ref[1, int(a == 0)])
        base = r4 * CLS4 + off
        st[3, base:base + UNIT, :] = num
        st[4, base:base + UNIT, :] = den
        st[5, base:base + UNIT, :] = mx

    def unit1(u):
        off = u * UNIT
        num, den, mx = _attn_unit(q1_ref[off:off + UNIT, :], kb1[off:off + 2 * UNIT, :],
                                  vb1[off:off + 2 * UNIT, :], bias_ref[0, int(u == 0)])
        for r4 in range(4):
            base = r4 * CLS4 + u * 32
            rows = slice(r4 * 32, (r4 + 1) * 32)
            st[0, base:base + 32, :] = num[rows]
            st[1, base:base + 32, :] = den[rows]
            st[2, base:base + 32, :] = mx[rows]

    def unit16(r16):
        num, den, mx = _attn_unit(q16_ref[r16], kb16[r16], vb16[r16], bias_ref[2, 1])
        base = (r16 % 4) * CLS4 + r16 // 4
        st[6, pl.ds(base, UNIT, stride=4), :] = num
        st[7, pl.ds(base, UNIT, stride=4), :] = den
        st[8, pl.ds(base, UNIT, stride=4), :] = mx

    def merge(r4, half):
        j0 = half * MERGE_ROWS
        rows = slice(r4 * CLS4 + j0, r4 * CLS4 + j0 + MERGE_ROWS)
        m1, m4, m16 = st[2, rows, :], st[5, rows, :], st[8, rows, :]
        mm = jnp.maximum(jnp.maximum(m1, m4), m16)
        w1, w4, w16 = jnp.exp2(m1 - mm), jnp.exp2(m4 - mm), jnp.exp2(m16 - mm)
        den = w1 * st[1, rows, :] + w4 * st[4, rows, :] + w16 * st[7, rows, :]
        num = w1 * st[0, rows, :] + w4 * st[3, rows, :] + w16 * st[6, rows, :]
        nat_ref[pl.ds(4 * j0 + r4, MERGE_ROWS, stride=4), :] = num / den

    n_halves = CLS4 // MERGE_ROWS
    units1_per_half = ATTN_TILE // UNIT // n_halves
    for half in range(n_halves):
        for u in range(half * units1_per_half, (half + 1) * units1_per_half):
            unit1(u)
        for r4 in range(4):
            if half == 0:
                for c in range(4):
                    unit16(r4 + 4 * c)
            unit4(r4, 2 * half)
            unit4(r4, 2 * half + 1)
            merge(r4, half)
    o_ref[...] = nat_ref[...].astype(BF16)


def _attention(slopes, q1, k1, v1, q4, k4, v4, q16, k16, v16):
    B, _, S, _ = q1.shape
    n_attn = S // ATTN_TILE
    s1_spec = pl.BlockSpec((None, None, ATTN_TILE, HEAD_DIM), lambda b, h, t: (b, h, t, 0))
    s4_spec = pl.BlockSpec((None, None, None, 4, CLS4, HEAD_DIM), lambda b, h, t: (b, h, t, 0, 0, 0))
    s16_spec = pl.BlockSpec((None, None, None, 16, UNIT, HEAD_DIM), lambda b, h, t: (b, h, t, 0, 0, 0))
    return pl.pallas_call(
        _attention_body,
        grid=(B, N_HEADS, n_attn),
        in_specs=[pl.BlockSpec(memory_space=pltpu.SMEM),
                  s1_spec, s1_spec, s1_spec, s4_spec, s4_spec, s4_spec, s16_spec, s16_spec, s16_spec],
        out_specs=pl.BlockSpec((None, ATTN_TILE, HEAD_DIM), lambda b, h, t: (b, t, h)),
        out_shape=jax.ShapeDtypeStruct((B, S, ATTN_WIDTH), BF16),
        scratch_shapes=[
            pltpu.VMEM((KB1_ROWS, HEAD_DIM), BF16),
            pltpu.VMEM((KB1_ROWS, 2 * HEAD_DIM), BF16),
            pltpu.VMEM((4, KB4_ROWS, HEAD_DIM), BF16),
            pltpu.VMEM((4, KB4_ROWS, 2 * HEAD_DIM), BF16),
            pltpu.VMEM((16, KB16_ROWS, HEAD_DIM), BF16),
            pltpu.VMEM((16, KB16_ROWS, 2 * HEAD_DIM), BF16),
            pltpu.VMEM((9, ATTN_TILE, HEAD_DIM), F32),
            pltpu.VMEM((3, 2, UNIT, 2 * UNIT), F32),
            pltpu.VMEM((ATTN_TILE, HEAD_DIM), F32),
        ],
        compiler_params=pltpu.CompilerParams(
            dimension_semantics=("arbitrary", "arbitrary", "arbitrary"),
            vmem_limit_bytes=VMEM_LIMIT_BYTES),
        name="dilated_attention",
    )(slopes, q1, k1, v1, q4, k4, v4, q16, k16, v16)


def _out_proj_body(x_ref, yp_ref, ya_ref, ga_ref, w_ref, gm_ref, x1_ref, h_ref):
    o = ya_ref[...].astype(F32)
    ya = (o * _rms_scale(o, ATTN_WIDTH) * ga_ref[...]).astype(BF16)
    z = jnp.dot(yp_ref[...], w_ref[0:POOL_WIDTH, :], preferred_element_type=F32)
    z = z + jnp.dot(ya, w_ref[POOL_WIDTH:, :], preferred_element_type=F32)
    x1 = x_ref[...] + z
    x1_ref[...] = x1
    h_ref[...] = (x1 * _rms_scale(x1, D_MODEL) * gm_ref[...]).astype(BF16)


def _out_proj(x, y_pool, y_attn, g_attn, w_out, g_mlp):
    B, S, _ = x.shape
    tm = TM_OUT
    const = lambda b, i: (0, 0)
    row = lambda width: pl.BlockSpec((None, tm, width), lambda b, i: (b, i, 0))
    return pl.pallas_call(
        _out_proj_body,
        grid=(B, S // tm),
        in_specs=[row(D_MODEL), row(POOL_WIDTH), row(ATTN_WIDTH),
                  pl.BlockSpec((1, ATTN_WIDTH), const),
                  pl.BlockSpec((POOL_WIDTH + ATTN_WIDTH, D_MODEL), const, pipeline_mode=pl.Buffered(1)),
                  pl.BlockSpec((1, D_MODEL), const)],
        out_specs=[row(D_MODEL), row(D_MODEL)],
        out_shape=[jax.ShapeDtypeStruct((B, S, D_MODEL), F32),
                   jax.ShapeDtypeStruct((B, S, D_MODEL), BF16)],
        compiler_params=pltpu.CompilerParams(
            dimension_semantics=("arbitrary", "arbitrary"), vmem_limit_bytes=VMEM_LIMIT_BYTES),
        name="out_proj",
    )(x, y_pool, y_attn, g_attn, w_out, g_mlp)


def _mlp_body(h_ref, x1_ref, wu_ref, wd_ref, gf_ref, o_ref):
    f = pl.program_id(1)

    def chunk():
        a = jnp.maximum(jnp.dot(h_ref[...], wu_ref[...], preferred_element_type=F32), 0.0)
        return jnp.dot((a * a).astype(BF16), wd_ref[...], preferred_element_type=F32)

    @pl.when(f == 0)
    def _():
        o_ref[...] = x1_ref[...] + chunk()

    @pl.when(f > 0)
    def _():
        o_ref[...] += chunk()

    @pl.when(f == pl.num_programs(1) - 1)
    def _():
        x2 = o_ref[...]
        o_ref[...] = x2 * _rms_scale(x2, D_MODEL) * gf_ref[...]


def _mlp(h, x1, w_up, w_down, g_final):
    M = h.shape[0]
    tm, tf = TM_MLP, TF_MLP
    return pl.pallas_call(
        _mlp_body,
        grid=(M // tm, D_FF // tf),
        in_specs=[pl.BlockSpec((tm, D_MODEL), lambda i, f: (i, 0)),
                  pl.BlockSpec((tm, D_MODEL), lambda i, f: (i, 0)),
                  pl.BlockSpec((D_MODEL, tf), lambda i, f: (0, f)),
                  pl.BlockSpec((tf, D_MODEL), lambda i, f: (f, 0)),
                  pl.BlockSpec((1, D_MODEL), lambda i, f: (0, 0))],
        out_specs=pl.BlockSpec((tm, D_MODEL), lambda i, f: (i, 0)),
        out_shape=jax.ShapeDtypeStruct((M, D_MODEL), F32),
        compiler_params=pltpu.CompilerParams(
            dimension_semantics=("arbitrary", "arbitrary"), vmem_limit_bytes=VMEM_LIMIT_BYTES),
        name="mlp",
    )(h, x1, w_up, w_down, g_final)


def kernel(x, norm_mix_g, w_in, pool_w, pool_scale, pool_out_norm_g, attn_out_norm_g,
           w_out, norm_mlp_g, w_up, w_down, norm_final_g):
    B, S, D = x.shape
    assert D == D_MODEL and S % ATTN_TILE == 0 and norm_mix_g.shape[0] == 1
    slopes = 2.0 ** (-8.0 * jnp.arange(1, N_HEADS + 1, dtype=F32) / N_HEADS)
    y_pool, q1, k1, v1, q4, k4, v4, q16, k16, v16 = _in_proj(
        x, norm_mix_g[0][None, :], w_in[0].astype(BF16), pool_w[0].astype(BF16),
        pool_scale[0][None, :], pool_out_norm_g[0][None, :])
    y_attn = _attention(slopes, q1, k1, v1, q4, k4, v4, q16, k16, v16)
    x1, h = _out_proj(x, y_pool, y_attn, attn_out_norm_g[0][None, :], w_out[0].astype(BF16),
                      norm_mlp_g[0][None, :])
    out = _mlp(h.reshape(B * S, D), x1.reshape(B * S, D), w_up[0].astype(BF16),
               w_down[0].astype(BF16), norm_final_g[None, :])
    return out.reshape(B, S, D)
```

```python
import jax
import jax.numpy as jnp
from jax import lax
from jax.experimental import pallas as pl
from jax.experimental.pallas import tpu as pltpu

F32 = jnp.float32
BF16 = jnp.bfloat16

D_MODEL = 2048
POOL_WIDTH = 1024
POOL_WINDOWS = (2, 4, 8, 16)
POOL_GROUP = 256
MAX_POOL_WINDOW = 16
ATTN_WIDTH = 1024
HEAD_DIM = 128
N_HEADS = 8
DILATIONS = (1, 4, 16)
WINDOW_STEPS = 128
ATTN_TILE = 2048
D_FF = 8192
NORM_EPS = 1e-6
LOG2_E = 1.4426950408889634

VMEM_LIMIT_BYTES = 63 * 1024 * 1024

TM_IN = 512
TM_OUT = 1024
SUB_OUT = 256
TM_MLP = 1024
TF_MLP = 1024
MERGE_ROWS = 256


def _rms_scale(v, width):
    return lax.rsqrt(jnp.sum(v * v, axis=-1, keepdims=True) * (1.0 / width) + NORM_EPS)


def _in_proj_body(x_ref, g_ref, w_ref, pw_ref, ps_ref, pg_ref,
                  ypool_ref, q1_ref, k1_ref, v1_ref, q4_ref, k4_ref, v4_ref,
                  q16_ref, k16_ref, v16_ref,
                  ext_ref, nat_ref, m4_ref):
    tm = TM_IN
    i = pl.program_id(1)

    @pl.when(i == 0)
    def _():
        ext_ref[0:MAX_POOL_WINDOW, :] = jnp.zeros((MAX_POOL_WINDOW, POOL_WIDTH), F32)

    halves = []
    for r0 in range(0, tm, tm // 2):
        xr = x_ref[r0:r0 + tm // 2, :]
        hr = (xr * _rms_scale(xr, D_MODEL) * g_ref[...]).astype(BF16)
        ext_ref[MAX_POOL_WINDOW + r0:MAX_POOL_WINDOW + r0 + tm // 2, :] = jnp.dot(
            hr, w_ref[:, 0:POOL_WIDTH], preferred_element_type=F32)
        halves.append(hr)
    h = jnp.concatenate(halves, axis=0)

    def pool_mixer():
        pos1 = (i * tm + lax.broadcasted_iota(jnp.int32, (tm, 1), 0) + 1).astype(F32)
        ys = []
        for g, w in enumerate(POOL_WINDOWS):
            cols = slice(g * POOL_GROUP, (g + 1) * POOL_GROUP)
            tr = ext_ref[:, cols]
            shift = 1
            while shift < w:
                tr = tr + pltpu.roll(tr, shift, 0)
                shift *= 2
            inv_cnt = 1.0 / jnp.minimum(pos1, float(w))
            d = tr[MAX_POOL_WINDOW:, :] * inv_cnt - ext_ref[MAX_POOL_WINDOW:, cols]
            y = jnp.dot(d.astype(BF16), pw_ref[g], preferred_element_type=F32)
            ys.append(y * ps_ref[:, cols])
        ext_ref[0:MAX_POOL_WINDOW, :] = ext_ref[tm:, :]
        ssq = jnp.sum(ys[0] * ys[0], axis=-1, keepdims=True)
        for y in ys[1:]:
            ssq = ssq + jnp.sum(y * y, axis=-1, keepdims=True)
        r = lax.rsqrt(ssq * (1.0 / POOL_WIDTH) + NORM_EPS)
        for g in range(len(POOL_WINDOWS)):
            cols = slice(g * POOL_GROUP, (g + 1) * POOL_GROUP)
            ypool_ref[:, cols] = (ys[g] * r * pg_ref[:, cols]).astype(BF16)

    def qkv(c, o1_ref, o4_ref, o16_ref):
        lo = POOL_WIDTH + c * ATTN_WIDTH
        pr = jnp.dot(h, w_ref[:, lo:lo + ATTN_WIDTH], preferred_element_type=F32)
        if c == 0:
            pr = pr * (HEAD_DIM ** -0.5 * LOG2_E)
        for hh in range(N_HEADS):
            blk = pr[:, hh * HEAD_DIM:(hh + 1) * HEAD_DIM]
            nat_ref[hh] = blk
            if c != 0:
                o1_ref[hh] = blk.astype(BF16)
        for hh in range(N_HEADS):
            if c == 0:
                for b128 in range(tm // 128):
                    for r4 in range(4):
                        rows = nat_ref[hh, pl.ds(b128 * 128 + r4, 32, stride=4), :]
                        o1_ref[hh, b128 * 128 + r4 * 32:b128 * 128 + (r4 + 1) * 32, :] = rows.astype(BF16)
            for r4 in range(4):
                rows = nat_ref[hh, pl.ds(r4, tm // 4, stride=4), :]
                o4_ref[hh, r4] = rows.astype(BF16)
                m4_ref[hh, r4] = rows
            for r16 in range(16):
                rows = m4_ref[hh, r16 % 4, pl.ds(r16 // 4, tm // 16, stride=4), :]
                o16_ref[hh, r16] = rows.astype(BF16)

    qkv(0, q1_ref, q4_ref, q16_ref)
    qkv(1, k1_ref, k4_ref, k16_ref)
    pool_mixer()
    qkv(2, v1_ref, v4_ref, v16_ref)


def _in_proj(x, g, w_in, pool_w, pool_scale, pool_g):
    B, S, _ = x.shape
    tm = TM_IN
    tiles_per_attn = ATTN_TILE // tm
    n_attn = S // ATTN_TILE
    const = lambda b, i: (0, 0)
    o1_spec = pl.BlockSpec((None, N_HEADS, tm, HEAD_DIM), lambda b, i: (b, 0, i, 0))
    o4_spec = pl.BlockSpec((None, N_HEADS, None, 4, tm // 4, HEAD_DIM),
                           lambda b, i: (b, 0, i // tiles_per_attn, 0, i % tiles_per_attn, 0))
    o16_spec = pl.BlockSpec((None, N_HEADS, None, 16, tm // 16, HEAD_DIM),
                            lambda b, i: (b, 0, i // tiles_per_attn, 0, i % tiles_per_attn, 0))
    s1 = jax.ShapeDtypeStruct((B, N_HEADS, S, HEAD_DIM), BF16)
    s4 = jax.ShapeDtypeStruct((B, N_HEADS, n_attn, 4, ATTN_TILE // 4, HEAD_DIM), BF16)
    s16 = jax.ShapeDtypeStruct((B, N_HEADS, n_attn, 16, ATTN_TILE // 16, HEAD_DIM), BF16)
    return pl.pallas_call(
        _in_proj_body,
        grid=(B, S // tm),
        in_specs=[
            pl.BlockSpec((None, tm, D_MODEL), lambda b, i: (b, i, 0)),
            pl.BlockSpec((1, D_MODEL), const),
            pl.BlockSpec((D_MODEL, POOL_WIDTH + 3 * ATTN_WIDTH), const, pipeline_mode=pl.Buffered(1)),
            pl.BlockSpec((len(POOL_WINDOWS), POOL_GROUP, POOL_GROUP), lambda b, i: (0, 0, 0),
                         pipeline_mode=pl.Buffered(1)),
            pl.BlockSpec((1, POOL_WIDTH), const),
            pl.BlockSpec((1, POOL_WIDTH), const),
        ],
        out_specs=[
            pl.BlockSpec((None, tm, POOL_WIDTH), lambda b, i: (b, i, 0)),
            o1_spec, o1_spec, o1_spec, o4_spec, o4_spec, o4_spec, o16_spec, o16_spec, o16_spec,
        ],
        out_shape=[jax.ShapeDtypeStruct((B, S, POOL_WIDTH), BF16),
                   s1, s1, s1, s4, s4, s4, s16, s16, s16],
        scratch_shapes=[
            pltpu.VMEM((tm + MAX_POOL_WINDOW, POOL_WIDTH), F32),
            pltpu.VMEM((N_HEADS, tm, HEAD_DIM), F32),
            pltpu.VMEM((N_HEADS, 4, tm // 4, HEAD_DIM), F32),
        ],
        compiler_params=pltpu.CompilerParams(
            dimension_semantics=("arbitrary", "arbitrary"), vmem_limit_bytes=VMEM_LIMIT_BYTES),
        name="in_proj",
    )(x, g, w_in, pool_w, pool_scale, pool_g)


KB1_ROWS = WINDOW_STEPS + ATTN_TILE
KB4_ROWS = WINDOW_STEPS + ATTN_TILE // 4
KB16_ROWS = WINDOW_STEPS + ATTN_TILE // 16
UNIT = 128
CLS4 = ATTN_TILE // 4


def _attn_unit(q, kk, vv, bias):
    s = lax.dot_general(q, kk, (((1,), (1,)), ((), ())), preferred_element_type=F32) + bias
    m = jnp.max(s, axis=-1, keepdims=True)
    p = jnp.exp2(s - m)
    acc = jnp.dot(p.astype(BF16), vv, preferred_element_type=F32)
    return acc[:, :HEAD_DIM], acc[:, HEAD_DIM:], jnp.broadcast_to(m, (UNIT, HEAD_DIM))


def _attention_body(slopes_ref, q1_ref, k1_ref, v1_ref, q4_ref, k4_ref, v4_ref,
                    q16_ref, k16_ref, v16_ref, o_ref,
                    kb1, vb1, kb4, vb4, kb16, vb16, st, bias_ref, nat_ref):
    h = pl.program_id(1)
    t = pl.program_id(2)
    ninf = -jnp.inf

    @pl.when(t == 0)
    def _():
        kb1[0:UNIT, :] = jnp.zeros((UNIT, HEAD_DIM), BF16)
        kb4[:, 0:UNIT, :] = jnp.zeros((4, UNIT, HEAD_DIM), BF16)
        kb16[:, 0:UNIT, :] = jnp.zeros((16, UNIT, HEAD_DIM), BF16)
        vb1[0:UNIT, 0:HEAD_DIM] = jnp.zeros((UNIT, HEAD_DIM), BF16)
        vb4[:, 0:UNIT, 0:HEAD_DIM] = jnp.zeros((4, UNIT, HEAD_DIM), BF16)
        vb16[:, 0:UNIT, 0:HEAD_DIM] = jnp.zeros((16, UNIT, HEAD_DIM), BF16)
        vb1[:, HEAD_DIM:] = jnp.ones((KB1_ROWS, HEAD_DIM), BF16)
        vb4[:, :, HEAD_DIM:] = jnp.ones((4, KB4_ROWS, HEAD_DIM), BF16)
        vb16[:, :, HEAD_DIM:] = jnp.ones((16, KB16_ROWS, HEAD_DIM), BF16)

    @pl.when(t > 0)
    def _():
        kb1[0:UNIT, :] = kb1[ATTN_TILE:, :]
        vb1[0:UNIT, 0:HEAD_DIM] = vb1[ATTN_TILE:, 0:HEAD_DIM]
        kb4[:, 0:UNIT, :] = kb4[:, CLS4:, :]
        vb4[:, 0:UNIT, 0:HEAD_DIM] = vb4[:, CLS4:, 0:HEAD_DIM]
        kb16[:, 0:UNIT, :] = kb16[:, UNIT:, :]
        vb16[:, 0:UNIT, 0:HEAD_DIM] = vb16[:, UNIT:, 0:HEAD_DIM]

    kb1[UNIT:, :] = k1_ref[...]
    vb1[UNIT:, 0:HEAD_DIM] = v1_ref[...]
    kb4[:, UNIT:, :] = k4_ref[...]
    vb4[:, UNIT:, 0:HEAD_DIM] = v4_ref[...]
    kb16[:, UNIT:, :] = k16_ref[...]
    vb16[:, UNIT:, 0:HEAD_DIM] = v16_ref[...]

    slope = slopes_ref[h]
    row = lax.broadcasted_iota(jnp.int32, (UNIT, 2 * UNIT), 0)
    col = lax.broadcasted_iota(jnp.int32, (UNIT, 2 * UNIT), 1)
    row_perm = (row % 32) * 4 + row // 32
    first_pen = jnp.where((col < UNIT) & (t == 0), ninf, 0.0).astype(F32)
    for p, dil in enumerate(DILATIONS):
        j = (row_perm if dil == 1 else row) + UNIT - col
        valid = (j >= 0) & (j <= WINDOW_STEPS)
        b = jnp.where(valid, -(slope * (float(dil) * LOG2_E)) * j.astype(F32), ninf)
        bias_ref[p, 0] = b
        bias_ref[p, 1] = b + first_pen

    def unit4(r4, a):
        off = a * UNIT
        num, den, mx = _attn_unit(q4_ref[r4, off:off + UNIT, :], kb4[r4, off:off + 2 * UNIT, :],
                                  vb4[r4, off:off + 2 * UNIT, :], bias_ref[1, int(a == 0)])
        base = r4 * CLS4 + off
        st[3, base:base + UNIT, :] = num
        st[4, base:base + UNIT, :] = den
        st[5, base:base + UNIT, :] = mx

    def unit1(u):
        off = u * UNIT
        num, den, mx = _attn_unit(q1_ref[off:off + UNIT, :], kb1[off:off + 2 * UNIT, :],
                                  vb1[off:off + 2 * UNIT, :], bias_ref[0, int(u == 0)])
        for r4 in range(4):
            base = r4 * CLS4 + u * 32
            rows = slice(r4 * 32, (r4 + 1) * 32)
            st[0, base:base + 32, :] = num[rows]
            st[1, base:base + 32, :] = den[rows]
            st[2, base:base + 32, :] = mx[rows]

    def unit16(r16):
        num, den, mx = _attn_unit(q16_ref[r16], kb16[r16], vb16[r16], bias_ref[2, 1])
        base = (r16 % 4) * CLS4 + r16 // 4
        st[6, pl.ds(base, UNIT, stride=4), :] = num
        st[7, pl.ds(base, UNIT, stride=4), :] = den
        st[8, pl.ds(base, UNIT, stride=4), :] = mx

    def merge(r4, half):
        j0 = half * MERGE_ROWS
        rows = slice(r4 * CLS4 + j0, r4 * CLS4 + j0 + MERGE_ROWS)
        m1, m4, m16 = st[2, rows, :], st[5, rows, :], st[8, rows, :]
        mm = jnp.maximum(jnp.maximum(m1, m4), m16)
        w1, w4, w16 = jnp.exp2(m1 - mm), jnp.exp2(m4 - mm), jnp.exp2(m16 - mm)
        den = w1 * st[1, rows, :] + w4 * st[4, rows, :] + w16 * st[7, rows, :]
        num = w1 * st[0, rows, :] + w4 * st[3, rows, :] + w16 * st[6, rows, :]
        nat_ref[pl.ds(4 * j0 + r4, MERGE_ROWS, stride=4), :] = num / den

    n_halves = CLS4 // MERGE_ROWS
    units1_per_half = ATTN_TILE // UNIT // n_halves
    for half in range(n_halves):
        for u in range(half * units1_per_half, (half + 1) * units1_per_half):
            unit1(u)
        for r4 in range(4):
            if half == 0:
                for c in range(4):
                    unit16(r4 + 4 * c)
            unit4(r4, 2 * half)
            unit4(r4, 2 * half + 1)
            merge(r4, half)
    o_ref[...] = nat_ref[...].astype(BF16)


def _attention(slopes, q1, k1, v1, q4, k4, v4, q16, k16, v16):
    B, _, S, _ = q1.shape
    n_attn = S // ATTN_TILE
    s1_spec = pl.BlockSpec((None, None, ATTN_TILE, HEAD_DIM), lambda b, h, t: (b, h, t, 0))
    s4_spec = pl.BlockSpec((None, None, None, 4, CLS4, HEAD_DIM), lambda b, h, t: (b, h, t, 0, 0, 0))
    s16_spec = pl.BlockSpec((None, None, None, 16, UNIT, HEAD_DIM), lambda b, h, t: (b, h, t, 0, 0, 0))
    return pl.pallas_call(
        _attention_body,
        grid=(B, N_HEADS, n_attn),
        in_specs=[pl.BlockSpec(memory_space=pltpu.SMEM),
                  s1_spec, s1_spec, s1_spec, s4_spec, s4_spec, s4_spec, s16_spec, s16_spec, s16_spec],
        out_specs=pl.BlockSpec((None, ATTN_TILE, HEAD_DIM), lambda b, h, t: (b, t, h)),
        out_shape=jax.ShapeDtypeStruct((B, S, ATTN_WIDTH), BF16),
        scratch_shapes=[
            pltpu.VMEM((KB1_ROWS, HEAD_DIM), BF16),
            pltpu.VMEM((KB1_ROWS, 2 * HEAD_DIM), BF16),
            pltpu.VMEM((4, KB4_ROWS, HEAD_DIM), BF16),
            pltpu.VMEM((4, KB4_ROWS, 2 * HEAD_DIM), BF16),
            pltpu.VMEM((16, KB16_ROWS, HEAD_DIM), BF16),
            pltpu.VMEM((16, KB16_ROWS, 2 * HEAD_DIM), BF16),
            pltpu.VMEM((9, ATTN_TILE, HEAD_DIM), F32),
            pltpu.VMEM((3, 2, UNIT, 2 * UNIT), F32),
            pltpu.VMEM((ATTN_TILE, HEAD_DIM), F32),
        ],
        compiler_params=pltpu.CompilerParams(
            dimension_semantics=("arbitrary", "arbitrary", "arbitrary"),
            vmem_limit_bytes=VMEM_LIMIT_BYTES),
        name="dilated_attention",
    )(slopes, q1, k1, v1, q4, k4, v4, q16, k16, v16)


def _out_proj_body(x_ref, yp_ref, ya_ref, ga_ref, w_ref, gm_ref, x1_ref, h_ref):
    for r0 in range(0, TM_OUT, SUB_OUT):
        rows = slice(r0, r0 + SUB_OUT)
        o = ya_ref[rows, :].astype(F32)
        ya = (o * _rms_scale(o, ATTN_WIDTH) * ga_ref[...]).astype(BF16)
        lhs = jnp.concatenate([yp_ref[rows, :], ya], axis=-1)
        x1 = x_ref[rows, :] + jnp.dot(lhs, w_ref[...], preferred_element_type=F32)
        x1_ref[rows, :] = x1
        h_ref[rows, :] = (x1 * _rms_scale(x1, D_MODEL) * gm_ref[...]).astype(BF16)


def _out_proj(x, y_pool, y_attn, g_attn, w_out, g_mlp):
    B, S, _ = x.shape
    tm = TM_OUT
    const = lambda b, i: (0, 0)
    row = lambda width: pl.BlockSpec((None, tm, width), lambda b, i: (b, i, 0))
    return pl.pallas_call(
        _out_proj_body,
        grid=(B, S // tm),
        in_specs=[row(D_MODEL), row(POOL_WIDTH), row(ATTN_WIDTH),
                  pl.BlockSpec((1, ATTN_WIDTH), const),
                  pl.BlockSpec((POOL_WIDTH + ATTN_WIDTH, D_MODEL), const, pipeline_mode=pl.Buffered(1)),
                  pl.BlockSpec((1, D_MODEL), const)],
        out_specs=[row(D_MODEL), row(D_MODEL)],
        out_shape=[jax.ShapeDtypeStruct((B, S, D_MODEL), F32),
                   jax.ShapeDtypeStruct((B, S, D_MODEL), BF16)],
        compiler_params=pltpu.CompilerParams(
            dimension_semantics=("arbitrary", "arbitrary"), vmem_limit_bytes=VMEM_LIMIT_BYTES),
        name="out_proj",
    )(x, y_pool, y_attn, g_attn, w_out, g_mlp)


def _mlp_body(h_ref, x1_ref, wu_ref, wd_ref, gf_ref, o_ref):
    f = pl.program_id(1)

    def chunk():
        a = jnp.maximum(jnp.dot(h_ref[...], wu_ref[...], preferred_element_type=F32), 0.0)
        return jnp.dot((a * a).astype(BF16), wd_ref[...], preferred_element_type=F32)

    @pl.when(f == 0)
    def _():
        o_ref[...] = x1_ref[...] + chunk()

    @pl.when(f > 0)
    def _():
        o_ref[...] += chunk()

    @pl.when(f == pl.num_programs(1) - 1)
    def _():
        x2 = o_ref[...]
        o_ref[...] = x2 * _rms_scale(x2, D_MODEL) * gf_ref[...]


def _mlp(h, x1, w_up, w_down, g_final):
    M = h.shape[0]
    tm, tf = TM_MLP, TF_MLP
    return pl.pallas_call(
        _mlp_body,
        grid=(M // tm, D_FF // tf),
        in_specs=[pl.BlockSpec((tm, D_MODEL), lambda i, f: (i, 0)),
                  pl.BlockSpec((tm, D_MODEL), lambda i, f: (i, 0)),
                  pl.BlockSpec((D_MODEL, tf), lambda i, f: (0, f)),
                  pl.BlockSpec((tf, D_MODEL), lambda i, f: (f, 0)),
                  pl.BlockSpec((1, D_MODEL), lambda i, f: (0, 0))],
        out_specs=pl.BlockSpec((tm, D_MODEL), lambda i, f: (i, 0)),
        out_shape=jax.ShapeDtypeStruct((M, D_MODEL), F32),
        compiler_params=pltpu.CompilerParams(
            dimension_semantics=("arbitrary", "arbitrary"), vmem_limit_bytes=VMEM_LIMIT_BYTES),
        name="mlp",
    )(h, x1, w_up, w_down, g_final)


def kernel(x, norm_mix_g, w_in, pool_w, pool_scale, pool_out_norm_g, attn_out_norm_g,
           w_out, norm_mlp_g, w_up, w_down, norm_final_g):
    B, S, D = x.shape
    assert D == D_MODEL and S % ATTN_TILE == 0 and norm_mix_g.shape[0] == 1
    slopes = 2.0 ** (-8.0 * jnp.arange(1, N_HEADS + 1, dtype=F32) / N_HEADS)
    y_pool, q1, k1, v1, q4, k4, v4, q16, k16, v16 = _in_proj(
        x, norm_mix_g[0][None, :], w_in[0].astype(BF16), pool_w[0].astype(BF16),
        pool_scale[0][None, :], pool_out_norm_g[0][None, :])
    y_attn = _attention(slopes, q1, k1, v1, q4, k4, v4, q16, k16, v16)
    x1, h = _out_proj(x, y_pool, y_attn, attn_out_norm_g[0][None, :], w_out[0].astype(BF16),
                      norm_mlp_g[0][None, :])
    out = _mlp(h.reshape(B * S, D), x1.reshape(B * S, D), w_up[0].astype(BF16),
               w_down[0].astype(BF16), norm_final_g[None, :])
    return out.reshape(B, S, D)
```

```python
import jax
import jax.numpy as jnp
from jax import lax
from jax.experimental import pallas as pl
from jax.experimental.pallas import tpu as pltpu

F32 = jnp.float32
BF16 = jnp.bfloat16

D_MODEL = 2048
POOL_WIDTH = 1024
POOL_WINDOWS = (2, 4, 8, 16)
POOL_GROUP = 256
MAX_POOL_WINDOW = 16
ATTN_WIDTH = 1024
HEAD_DIM = 128
N_HEADS = 8
DILATIONS = (1, 4, 16)
WINDOW_STEPS = 128
ATTN_TILE = 2048
D_FF = 8192
NORM_EPS = 1e-6
LOG2_E = 1.4426950408889634

VMEM_LIMIT_BYTES = 63 * 1024 * 1024

TM_IN = 512
TM_OUT = 512
TM_MLP = 1024
TF_MLP = 1024
MERGE_ROWS = 256


def _rms_scale(v, width):
    return lax.rsqrt(jnp.sum(v * v, axis=-1, keepdims=True) * (1.0 / width) + NORM_EPS)


def _in_proj_body(x_ref, g_ref, w_ref, pw_ref, ps_ref, pg_ref,
                  ypool_ref, q4_ref, k4_ref, v4_ref, q16_ref, k16_ref, v16_ref,
                  ext_ref, nat_ref, m4_ref):
    tm = TM_IN
    i = pl.program_id(1)

    @pl.when(i == 0)
    def _():
        ext_ref[0:MAX_POOL_WINDOW, :] = jnp.zeros((MAX_POOL_WINDOW, POOL_WIDTH), F32)

    halves = []
    for r0 in range(0, tm, tm // 2):
        xr = x_ref[r0:r0 + tm // 2, :]
        hr = (xr * _rms_scale(xr, D_MODEL) * g_ref[...]).astype(BF16)
        ext_ref[MAX_POOL_WINDOW + r0:MAX_POOL_WINDOW + r0 + tm // 2, :] = jnp.dot(
            hr, w_ref[:, 0:POOL_WIDTH], preferred_element_type=F32)
        halves.append(hr)
    h = jnp.concatenate(halves, axis=0)

    def pool_mixer():
        pos1 = (i * tm + lax.broadcasted_iota(jnp.int32, (tm, 1), 0) + 1).astype(F32)
        ys = []
        for g, w in enumerate(POOL_WINDOWS):
            cols = slice(g * POOL_GROUP, (g + 1) * POOL_GROUP)
            tr = ext_ref[:, cols]
            shift = 1
            while shift < w:
                tr = tr + pltpu.roll(tr, shift, 0)
                shift *= 2
            inv_cnt = 1.0 / jnp.minimum(pos1, float(w))
            d = tr[MAX_POOL_WINDOW:, :] * inv_cnt - ext_ref[MAX_POOL_WINDOW:, cols]
            y = jnp.dot(d.astype(BF16), pw_ref[g], preferred_element_type=F32)
            ys.append(y * ps_ref[:, cols])
        ext_ref[0:MAX_POOL_WINDOW, :] = ext_ref[tm:, :]
        ssq = jnp.sum(ys[0] * ys[0], axis=-1, keepdims=True)
        for y in ys[1:]:
            ssq = ssq + jnp.sum(y * y, axis=-1, keepdims=True)
        r = lax.rsqrt(ssq * (1.0 / POOL_WIDTH) + NORM_EPS)
        for g in range(len(POOL_WINDOWS)):
            cols = slice(g * POOL_GROUP, (g + 1) * POOL_GROUP)
            ypool_ref[:, cols] = (ys[g] * r * pg_ref[:, cols]).astype(BF16)

    def qkv(c, o4_ref, o16_ref):
        lo = POOL_WIDTH + c * ATTN_WIDTH
        pr = jnp.dot(h, w_ref[:, lo:lo + ATTN_WIDTH], preferred_element_type=F32)
        if c == 0:
            pr = pr * (HEAD_DIM ** -0.5 * LOG2_E)
        for hh in range(N_HEADS):
            nat_ref[hh] = pr[:, hh * HEAD_DIM:(hh + 1) * HEAD_DIM]
        for hh in range(N_HEADS):
            for r4 in range(4):
                rows = nat_ref[hh, pl.ds(r4, tm // 4, stride=4), :]
                o4_ref[hh, r4] = rows.astype(BF16)
                m4_ref[hh, r4] = rows
            for r16 in range(16):
                rows = m4_ref[hh, r16 % 4, pl.ds(r16 // 4, tm // 16, stride=4), :]
                o16_ref[hh, r16] = rows.astype(BF16)

    qkv(0, q4_ref, q16_ref)
    qkv(1, k4_ref, k16_ref)
    pool_mixer()
    qkv(2, v4_ref, v16_ref)


def _in_proj(x, g, w_in, pool_w, pool_scale, pool_g):
    B, S, _ = x.shape
    tm = TM_IN
    tiles_per_attn = ATTN_TILE // tm
    n_attn = S // ATTN_TILE
    const = lambda b, i: (0, 0)
    o4_spec = pl.BlockSpec((None, N_HEADS, None, 4, tm // 4, HEAD_DIM),
                           lambda b, i: (b, 0, i // tiles_per_attn, 0, i % tiles_per_attn, 0))
    o16_spec = pl.BlockSpec((None, N_HEADS, None, 16, tm // 16, HEAD_DIM),
                            lambda b, i: (b, 0, i // tiles_per_attn, 0, i % tiles_per_attn, 0))
    s4 =jax.ShapeDtypeStruct((B, N_HEADS, n_attn, 4, ATTN_TILE // 4, HEAD_DIM), BF16)
    s16 = jax.ShapeDtypeStruct((B, N_HEADS, n_attn, 16, ATTN_TILE // 16, HEAD_DIM), BF16)
    return pl.pallas_call(
        _in_proj_body,
        grid=(B, S // tm),
        in_specs=[
            pl.BlockSpec((None, tm, D_MODEL), lambda b, i: (b, i, 0)),
            pl.BlockSpec((1, D_MODEL), const),
            pl.BlockSpec((D_MODEL, POOL_WIDTH + 3 * ATTN_WIDTH), const, pipeline_mode=pl.Buffered(1)),
            pl.BlockSpec((len(POOL_WINDOWS), POOL_GROUP, POOL_GROUP), lambda b, i: (0, 0, 0),
                         pipeline_mode=pl.Buffered(1)),
            pl.BlockSpec((1, POOL_WIDTH), const),
            pl.BlockSpec((1, POOL_WIDTH), const),
        ],
        out_specs=[
            pl.BlockSpec((None, tm, POOL_WIDTH), lambda b, i: (b, i, 0)),
            o4_spec, o4_spec, o4_spec, o16_spec, o16_spec, o16_spec,
        ],
        out_shape=[jax.ShapeDtypeStruct((B, S, POOL_WIDTH), BF16), s4, s4, s4, s16, s16, s16],
        scratch_shapes=[
            pltpu.VMEM((tm + MAX_POOL_WINDOW, POOL_WIDTH), F32),
            pltpu.VMEM((N_HEADS, tm, HEAD_DIM), F32),
            pltpu.VMEM((N_HEADS, 4, tm // 4, HEAD_DIM), F32),
        ],
        compiler_params=pltpu.CompilerParams(
            dimension_semantics=("arbitrary", "arbitrary"), vmem_limit_bytes=VMEM_LIMIT_BYTES),
        name="in_proj",
    )(x, g, w_in, pool_w, pool_scale, pool_g)


KB4_ROWS = WINDOW_STEPS + ATTN_TILE // 4
KB16_ROWS = WINDOW_STEPS + ATTN_TILE // 16
UNIT = 128
CLS4 = ATTN_TILE // 4
Q1_ROWS = UNIT // 4
K1_ROWS = 2 * Q1_ROWS


def _attn_unit(q, kk, vv, bias):
    s = lax.dot_general(q, kk, (((1,), (1,)), ((), ())), preferred_element_type=F32) + bias
    m = jnp.max(s, axis=-1, keepdims=True)
    p = jnp.exp2(s - m)
    acc = jnp.dot(p.astype(BF16), vv, preferred_element_type=F32)
    return acc[:, :HEAD_DIM], acc[:, HEAD_DIM:], jnp.broadcast_to(m, (UNIT, HEAD_DIM))


def _attention_body(slopes_ref, q4_ref, k4_ref, v4_ref, q16_ref, k16_ref, v16_ref, o_ref,
                    kb4, vb4, kb16, vb16, st, bias_ref, nat_ref):
    h = pl.program_id(1)
    t = pl.program_id(2)
    ninf = -jnp.inf

    @pl.when(t == 0)
    def _():
        kb4[:, 0:UNIT, :] = jnp.zeros((4, UNIT, HEAD_DIM), BF16)
        kb16[:, 0:UNIT, :] = jnp.zeros((16, UNIT, HEAD_DIM), BF16)
        vb4[:, 0:UNIT, 0:HEAD_DIM] = jnp.zeros((4, UNIT, HEAD_DIM), BF16)
        vb16[:, 0:UNIT, 0:HEAD_DIM] = jnp.zeros((16, UNIT, HEAD_DIM), BF16)
        vb4[:, :, HEAD_DIM:] = jnp.ones((4, KB4_ROWS, HEAD_DIM), BF16)
        vb16[:, :, HEAD_DIM:] = jnp.ones((16, KB16_ROWS, HEAD_DIM), BF16)

    @pl.when(t > 0)
    def _():
        kb4[:, 0:UNIT, :] = kb4[:, CLS4:, :]
        vb4[:, 0:UNIT, 0:HEAD_DIM] = vb4[:, CLS4:, 0:HEAD_DIM]
        kb16[:, 0:UNIT, :] = kb16[:, UNIT:, :]
        vb16[:, 0:UNIT, 0:HEAD_DIM] = vb16[:, UNIT:, 0:HEAD_DIM]

    kb4[:, UNIT:, :] = k4_ref[...]
    vb4[:, UNIT:, 0:HEAD_DIM] = v4_ref[...]
    kb16[:, UNIT:, :] = k16_ref[...]
    vb16[:, UNIT:, 0:HEAD_DIM] = v16_ref[...]

    slope = slopes_ref[h]
    row = lax.broadcasted_iota(jnp.int32, (UNIT, 2 * UNIT), 0)
    col = lax.broadcasted_iota(jnp.int32, (UNIT, 2 * UNIT), 1)
    for p, dil in enumerate(DILATIONS):
        if dil == 1:
            j = 4 * (row % Q1_ROWS) + row // Q1_ROWS - 4 * (col % K1_ROWS) - col // K1_ROWS + UNIT
            lookback = (col % K1_ROWS) < Q1_ROWS
        else:
            j = row + UNIT - col
            lookback = col < UNIT
        valid = (j >= 0) & (j <= WINDOW_STEPS)
        b = jnp.where(valid, -(slope * (float(dil) * LOG2_E)) * j.astype(F32), ninf)
        bias_ref[p, 0] = b
        bias_ref[p, 1] = b + jnp.where(lookback & (t == 0), ninf, 0.0).astype(F32)

    def unit4(r4, a):
        off = a * UNIT
        num, den, mx = _attn_unit(q4_ref[r4, off:off + UNIT, :], kb4[r4, off:off + 2 * UNIT, :],
                                  vb4[r4, off:off + 2 * UNIT, :], bias_ref[1, int(a == 0)])
        base = r4 * CLS4 + off
        st[3, base:base + UNIT, :] = num
        st[4, base:base + UNIT, :] = den
        st[5, base:base + UNIT, :] = mx

    def unit1(u):
        q0 = u * Q1_ROWS
        k0 = UNIT + q0 - Q1_ROWS
        q = jnp.concatenate([q4_ref[r4, q0:q0 + Q1_ROWS, :] for r4 in range(4)], axis=0)
        kk = jnp.concatenate([kb4[r4, k0:k0 + K1_ROWS, :] for r4 in range(4)], axis=0)
        vv = jnp.concatenate([vb4[r4, k0:k0 + K1_ROWS, :] for r4 in range(4)], axis=0)
        num, den, mx = _attn_unit(q, kk, vv, bias_ref[0, int(u == 0)])
        for r4 in range(4):
            base = r4 * CLS4 + u * 32
            rows = slice(r4 * 32, (r4 + 1) * 32)
            st[0, base:base + 32, :] = num[rows]
            st[1, base:base + 32, :] = den[rows]
            st[2, base:base + 32, :] = mx[rows]

    def unit16(r16):
        num, den, mx = _attn_unit(q16_ref[r16], kb16[r16], vb16[r16], bias_ref[2, 1])
        base = (r16 % 4) * CLS4 + r16 // 4
        st[6, pl.ds(base, UNIT, stride=4), :] = num
        st[7, pl.ds(base, UNIT, stride=4), :] = den
        st[8, pl.ds(base, UNIT, stride=4), :] = mx

    def merge(r4, half):
        j0 = half * MERGE_ROWS
        rows = slice(r4 * CLS4 + j0, r4 * CLS4 + j0 + MERGE_ROWS)
        m1, m4, m16 = st[2, rows, :], st[5, rows, :], st[8, rows, :]
        mm = jnp.maximum(jnp.maximum(m1, m4), m16)
        w1, w4, w16 = jnp.exp2(m1 - mm), jnp.exp2(m4 - mm), jnp.exp2(m16 - mm)
        den = w1 * st[1, rows, :] + w4 * st[4, rows, :] + w16 * st[7, rows, :]
        num = w1 * st[0, rows, :] + w4 * st[3, rows, :] + w16 * st[6, rows, :]
        nat_ref[pl.ds(4 * j0 + r4, MERGE_ROWS, stride=4), :] = num / den

    n_halves = CLS4 // MERGE_ROWS
    units1_per_half = ATTN_TILE // UNIT // n_halves
    for half in range(n_halves):
        for u in range(half * units1_per_half, (half + 1) * units1_per_half):
            unit1(u)
        for r4 in range(4):
            if half == 0:
                for c in range(4):
                    unit16(r4 + 4 * c)
            unit4(r4, 2 * half)
            unit4(r4, 2 * half + 1)
            merge(r4, half)
    o_ref[...] = nat_ref[...].astype(BF16)


def _attention(slopes, q4, k4, v4, q16, k16, v16):
    B, _, n_attn = q4.shape[:3]
    S = n_attn * ATTN_TILE
    s4_spec = pl.BlockSpec((None, None, None, 4, CLS4, HEAD_DIM), lambda b, h, t: (b, h, t, 0, 0, 0))
    s16_spec = pl.BlockSpec((None, None, None, 16, UNIT, HEAD_DIM), lambda b, h, t: (b, h, t, 0, 0, 0))
    return pl.pallas_call(
        _attention_body,
        grid=(B, N_HEADS, n_attn),
        in_specs=[pl.BlockSpec(memory_space=pltpu.SMEM),
                  s4_spec, s4_spec, s4_spec, s16_spec, s16_spec, s16_spec],
        out_specs=pl.BlockSpec((None, ATTN_TILE, HEAD_DIM), lambda b, h, t: (b, t, h)),
        out_shape=jax.ShapeDtypeStruct((B, S, ATTN_WIDTH), BF16),
        scratch_shapes=[
            pltpu.VMEM((4, KB4_ROWS, HEAD_DIM), BF16),
            pltpu.VMEM((4, KB4_ROWS, 2 * HEAD_DIM), BF16),
            pltpu.VMEM((16, KB16_ROWS, HEAD_DIM), BF16),
            pltpu.VMEM((16, KB16_ROWS, 2 * HEAD_DIM), BF16),
            pltpu.VMEM((9, ATTN_TILE, HEAD_DIM), F32),
            pltpu.VMEM((3, 2, UNIT, 2 * UNIT), F32),
            pltpu.VMEM((ATTN_TILE, HEAD_DIM), F32),
        ],
        compiler_params=pltpu.CompilerParams(
            dimension_semantics=("arbitrary", "arbitrary", "arbitrary"),
            vmem_limit_bytes=VMEM_LIMIT_BYTES),
        name="dilated_attention",
    )(slopes, q4, k4, v4, q16, k16, v16)


def _out_proj_body(x_ref, yp_ref, ya_ref, ga_ref, w_ref, gm_ref, x1_ref, h_ref):
    o = ya_ref[...].astype(F32)
    ya = (o * _rms_scale(o, ATTN_WIDTH) * ga_ref[...]).astype(BF16)
    z = jnp.dot(yp_ref[...], w_ref[0:POOL_WIDTH, :], preferred_element_type=F32)
    z = z + jnp.dot(ya, w_ref[POOL_WIDTH:, :], preferred_element_type=F32)
    x1 = x_ref[...] + z
    x1_ref[...] = x1
    h_ref[...] = (x1 * _rms_scale(x1, D_MODEL) * gm_ref[...]).astype(BF16)


def _out_proj(x, y_pool, y_attn, g_attn, w_out, g_mlp):
    B, S, _ = x.shape
    tm = TM_OUT
    const = lambda b, i: (0, 0)
    row = lambda width: pl.BlockSpec((None, tm, width), lambda b, i: (b, i, 0))
    return pl.pallas_call(
        _out_proj_body,
        grid=(B, S // tm),
        in_specs=[row(D_MODEL), row(POOL_WIDTH), row(ATTN_WIDTH),
                  pl.BlockSpec((1, ATTN_WIDTH), const),
                  pl.BlockSpec((POOL_WIDTH + ATTN_WIDTH, D_MODEL), const, pipeline_mode=pl.Buffered(1)),
                  pl.BlockSpec((1, D_MODEL), const)],
        out_specs=[row(D_MODEL), row(D_MODEL)],
        out_shape=[jax.ShapeDtypeStruct((B, S, D_MODEL), F32),
                   jax.ShapeDtypeStruct((B, S, D_MODEL), BF16)],
        compiler_params=pltpu.CompilerParams(
            dimension_semantics=("arbitrary", "arbitrary"), vmem_limit_bytes=VMEM_LIMIT_BYTES),
        name="out_proj",
    )(x, y_pool, y_attn, g_attn, w_out, g_mlp)


def _mlp_body(h_ref, x1_ref, wu_ref, wd_ref, gf_ref, o_ref):
    f = pl.program_id(1)

    def chunk():
        a = jnp.maximum(jnp.dot(h_ref[...], wu_ref[...], preferred_element_type=F32), 0.0)
        return jnp.dot((a * a).astype(BF16), wd_ref[...], preferred_element_type=F32)

    @pl.when(f == 0)
    def _():
        o_ref[...] = x1_ref[...] + chunk()

    @pl.when(f > 0)
    def _():
        o_ref[...] += chunk()

    @pl.when(f == pl.num_programs(1) - 1)
    def _():
        x2 = o_ref[...]
        o_ref[...] = x2 * _rms_scale(x2, D_MODEL) * gf_ref[...]


def _mlp(h, x1, w_up, w_down, g_final):
    M = h.shape[0]
    tm, tf = TM_MLP, TF_MLP
    return pl.pallas_call(
        _mlp_body,
        grid=(M // tm, D_FF // tf),
        in_specs=[pl.BlockSpec((tm, D_MODEL), lambda i, f: (i, 0)),
                  pl.BlockSpec((tm, D_MODEL), lambda i, f: (i, 0)),
                  pl.BlockSpec((D_MODEL, tf), lambda i, f: (0, f)),
                  pl.BlockSpec((tf, D_MODEL), lambda i, f: (f, 0)),
                  pl.BlockSpec((1, D_MODEL), lambda i, f: (0, 0))],
        out_specs=pl.BlockSpec((tm, D_MODEL), lambda i, f: (i, 0)),
        out_shape=jax.ShapeDtypeStruct((M, D_MODEL), F32),
        compiler_params=pltpu.CompilerParams(
            dimension_semantics=("arbitrary", "arbitrary"), vmem_limit_bytes=VMEM_LIMIT_BYTES),
        name="mlp",
    )(h, x1, w_up, w_down, g_final)


def kernel(x, norm_mix_g, w_in, pool_w, pool_scale, pool_out_norm_g, attn_out_norm_g,
           w_out, norm_mlp_g, w_up, w_down, norm_final_g):
    B, S, D = x.shape
    assert D == D_MODEL and S % ATTN_TILE == 0 and norm_mix_g.shape[0] == 1
    slopes = 2.0 ** (-8.0 * jnp.arange(1, N_HEADS + 1, dtype=F32) / N_HEADS)
    y_pool, q4, k4, v4, q16, k16, v16 = _in_proj(
        x, norm_mix_g[0][None, :], w_in[0].astype(BF16), pool_w[0].astype(BF16),
        pool_scale[0][None, :], pool_out_norm_g[0][None, :])
    y_attn = _attention(slopes, q4, k4, v4, q16, k16, v16)
    x1, h = _out_proj(x, y_pool, y_attn, attn_out_norm_g[0][None, :], w_out[0].astype(BF16),
                      norm_mlp_g[0][None, :])
    out = _mlp(h.reshape(B * S, D), x1.reshape(B * S, D), w_up[0].astype(BF16),
               w_down[0].astype(BF16), norm_final_g[None, :])
    return out.reshape(B, S, D)
```

```python
import jax
import jax.numpy as jnp
from jax import lax
from jax.experimental import pallas as pl
from jax.experimental.pallas import tpu as pltpu

F32 = jnp.float32
BF16 = jnp.bfloat16

D_MODEL = 2048
POOL_WIDTH = 1024
POOL_WINDOWS = (2, 4, 8, 16)
POOL_GROUP = 256
MAX_POOL_WINDOW = 16
ATTN_WIDTH = 1024
HEAD_DIM = 128
N_HEADS = 8
DILATIONS = (1, 4, 16)
WINDOW_STEPS = 128
ATTN_TILE = 2048
D_FF = 8192
NORM_EPS = 1e-6
LOG2_E = 1.4426950408889634

VMEM_LIMIT_BYTES = 63 * 1024 * 1024

TM_IN = 512
TM_OUT = 512
TM_MLP = 1024
TF_MLP = 1024


def _rms_scale(v, width):
    return lax.rsqrt(jnp.sum(v * v, axis=-1, keepdims=True) * (1.0 / width) + NORM_EPS)


def _in_proj_body(x_ref, g_ref, w_ref, pw_ref, ps_ref, pg_ref,
                  ypool_ref, q4_ref, k4_ref, v4_ref, q16_ref, k16_ref, v16_ref,
                  ext_ref, nat_ref, m4_ref):
    tm = TM_IN
    i = pl.program_id(1)

    @pl.when(i == 0)
    def _():
        ext_ref[0:MAX_POOL_WINDOW, :] = jnp.zeros((MAX_POOL_WINDOW, POOL_WIDTH), F32)

    halves = []
    for r0 in range(0, tm, tm // 2):
        xr = x_ref[r0:r0 + tm // 2, :]
        hr = (xr * _rms_scale(xr, D_MODEL) * g_ref[...]).astype(BF16)
        ext_ref[MAX_POOL_WINDOW + r0:MAX_POOL_WINDOW + r0 + tm // 2, :] = jnp.dot(
            hr, w_ref[:, 0:POOL_WIDTH], preferred_element_type=F32)
        halves.append(hr)
    h = jnp.concatenate(halves, axis=0)

    def pool_mixer():
        pos1 = (i * tm + lax.broadcasted_iota(jnp.int32, (tm, 1), 0) + 1).astype(F32)
        ys = []
        for g, w in enumerate(POOL_WINDOWS):
            cols = slice(g * POOL_GROUP, (g + 1) * POOL_GROUP)
            tr = ext_ref[:, cols]
            shift = 1
            while shift < w:
                tr = tr + pltpu.roll(tr, shift, 0)
                shift *= 2
            inv_cnt = 1.0 / jnp.minimum(pos1, float(w))
            d = tr[MAX_POOL_WINDOW:, :] * inv_cnt - ext_ref[MAX_POOL_WINDOW:, cols]
            y = jnp.dot(d.astype(BF16), pw_ref[g], preferred_element_type=F32)
            ys.append(y * ps_ref[:, cols])
        ext_ref[0:MAX_POOL_WINDOW, :] = ext_ref[tm:, :]
        ssq = jnp.sum(ys[0] * ys[0], axis=-1, keepdims=True)
        for y in ys[1:]:
            ssq = ssq + jnp.sum(y * y, axis=-1, keepdims=True)
        r = lax.rsqrt(ssq * (1.0 / POOL_WIDTH) + NORM_EPS)
        for g in range(len(POOL_WINDOWS)):
            cols = slice(g * POOL_GROUP, (g + 1) * POOL_GROUP)
            ypool_ref[:, cols] = (ys[g] * r * pg_ref[:, cols]).astype(BF16)

    def qkv(c, o4_ref, o16_ref):
        lo = POOL_WIDTH + c * ATTN_WIDTH
        pr = jnp.dot(h, w_ref[:, lo:lo + ATTN_WIDTH], preferred_element_type=F32)
        if c == 0:
            pr = pr * (HEAD_DIM ** -0.5 * LOG2_E)
        for hh in range(N_HEADS):
            nat_ref[hh] = pr[:, hh * HEAD_DIM:(hh + 1) * HEAD_DIM]
        for hh in range(N_HEADS):
            for r4 in range(4):
                rows = nat_ref[hh, pl.ds(r4, tm // 4, stride=4), :]
                o4_ref[hh, r4] = rows.astype(BF16)
                m4_ref[hh, r4] = rows
            for r16 in range(16):
                rows = m4_ref[hh, r16 % 4, pl.ds(r16 // 4, tm // 16, stride=4), :]
                o16_ref[hh, r16] = rows.astype(BF16)

    qkv(0, q4_ref, q16_ref)
    qkv(1, k4_ref, k16_ref)
    pool_mixer()
    qkv(2, v4_ref, v16_ref)


def _in_proj(x, g, w_in, pool_w, pool_scale, pool_g):
    B, S, _ = x.shape
    tm = TM_IN
    tiles_per_attn = ATTN_TILE // tm
    n_attn = S // ATTN_TILE
    const = lambda b, i: (0, 0)
    o4_spec = pl.BlockSpec((None, N_HEADS, None, 4, tm // 4, HEAD_DIM),
                           lambda b, i: (b, 0, i // tiles_per_attn, 0, i % tiles_per_attn, 0))
    o16_spec = pl.BlockSpec((None, N_HEADS, None, 16, tm // 16, HEAD_DIM),
                            lambda b, i: (b, 0, i // tiles_per_attn, 0, i % tiles_per_attn, 0))
    s4 =jax.ShapeDtypeStruct((B, N_HEADS, n_attn, 4, ATTN_TILE // 4, HEAD_DIM), BF16)
    s16 = jax.ShapeDtypeStruct((B, N_HEADS, n_attn, 16, ATTN_TILE // 16, HEAD_DIM), BF16)
    return pl.pallas_call(
        _in_proj_body,
        grid=(B, S // tm),
        in_specs=[
            pl.BlockSpec((None, tm, D_MODEL), lambda b, i: (b, i, 0)),
            pl.BlockSpec((1, D_MODEL), const),
            pl.BlockSpec((D_MODEL, POOL_WIDTH + 3 * ATTN_WIDTH), const, pipeline_mode=pl.Buffered(1)),
            pl.BlockSpec((len(POOL_WINDOWS), POOL_GROUP, POOL_GROUP), lambda b, i: (0, 0, 0),
                         pipeline_mode=pl.Buffered(1)),
            pl.BlockSpec((1, POOL_WIDTH), const),
            pl.BlockSpec((1, POOL_WIDTH), const),
        ],
        out_specs=[
            pl.BlockSpec((None, tm, POOL_WIDTH), lambda b, i: (b, i, 0)),
            o4_spec, o4_spec, o4_spec, o16_spec, o16_spec, o16_spec,
        ],
        out_shape=[jax.ShapeDtypeStruct((B, S, POOL_WIDTH), BF16), s4, s4, s4, s16, s16, s16],
        scratch_shapes=[
            pltpu.VMEM((tm + MAX_POOL_WINDOW, POOL_WIDTH), F32),
            pltpu.VMEM((N_HEADS, tm, HEAD_DIM), F32),
            pltpu.VMEM((N_HEADS, 4, tm // 4, HEAD_DIM), F32),
        ],
        compiler_params=pltpu.CompilerParams(
            dimension_semantics=("arbitrary", "arbitrary"), vmem_limit_bytes=VMEM_LIMIT_BYTES),
        name="in_proj",
    )(x, g, w_in, pool_w, pool_scale, pool_g)


KB4_ROWS = WINDOW_STEPS + ATTN_TILE // 4
KB16_ROWS = WINDOW_STEPS + ATTN_TILE // 16
UNIT = 128
CLS4 = ATTN_TILE // 4
Q1_ROWS = UNIT // 4
K1_ROWS = 2 * Q1_ROWS


def _attn_unit(q, kk, vv, bias):
    s = lax.dot_general(q, kk, (((1,), (1,)), ((), ())), preferred_element_type=F32) + bias
    m = jnp.max(s, axis=-1, keepdims=True)
    p = jnp.exp2(s - m)
    acc = jnp.dot(p.astype(BF16), vv, preferred_element_type=F32)
    return acc[:, :HEAD_DIM], acc[:, HEAD_DIM:], jnp.broadcast_to(m, (UNIT, HEAD_DIM))


def _attention_body(slopes_ref, q4_ref, k4_ref, v4_ref, q16_ref, k16_ref, v16_ref, o_ref,
                    kb4, vb4, kb16, vb16, st, bias_ref, nat_ref):
    h = pl.program_id(1)
    t = pl.program_id(2)
    ninf = -jnp.inf

    @pl.when(t == 0)
    def _():
        kb4[:, 0:UNIT, :] = jnp.zeros((4, UNIT, HEAD_DIM), BF16)
        kb16[:, 0:UNIT, :] = jnp.zeros((16, UNIT, HEAD_DIM), BF16)
        vb4[:, 0:UNIT, 0:HEAD_DIM] = jnp.zeros((4, UNIT, HEAD_DIM), BF16)
        vb16[:, 0:UNIT, 0:HEAD_DIM] = jnp.zeros((16, UNIT, HEAD_DIM), BF16)
        vb4[:, :, HEAD_DIM:] = jnp.ones((4, KB4_ROWS, HEAD_DIM), BF16)
        vb16[:, :, HEAD_DIM:] = jnp.ones((16, KB16_ROWS, HEAD_DIM), BF16)

    @pl.when(t > 0)
    def _():
        kb4[:, 0:UNIT, :] = kb4[:, CLS4:, :]
        vb4[:, 0:UNIT, 0:HEAD_DIM] = vb4[:, CLS4:, 0:HEAD_DIM]
        kb16[:, 0:UNIT, :] = kb16[:, UNIT:, :]
        vb16[:, 0:UNIT, 0:HEAD_DIM] = vb16[:, UNIT:, 0:HEAD_DIM]

    kb4[:, UNIT:, :] = k4_ref[...]
    vb4[:, UNIT:, 0:HEAD_DIM] = v4_ref[...]
    kb16[:, UNIT:, :] = k16_ref[...]
    vb16[:, UNIT:, 0:HEAD_DIM] = v16_ref[...]

    slope = slopes_ref[h]
    row = lax.broadcasted_iota(jnp.int32, (UNIT, 2 * UNIT), 0)
    col = lax.broadcasted_iota(jnp.int32, (UNIT, 2 * UNIT), 1)
    for p, dil in enumerate(DILATIONS):
        if dil == 1:
            j = 4 * (row % Q1_ROWS) + row // Q1_ROWS - 4 * (col % K1_ROWS) - col // K1_ROWS + UNIT
            lookback = (col % K1_ROWS) < Q1_ROWS
        else:
            j = row + UNIT - col
            lookback = col < UNIT
        valid = (j >= 0) & (j <= WINDOW_STEPS)
        b = jnp.where(valid, -(slope * (float(dil) * LOG2_E)) * j.astype(F32), ninf)
        bias_ref[p, 0] = b
        bias_ref[p, 1] = b + jnp.where(lookback & (t == 0), ninf, 0.0).astype(F32)

    def unit4_merge(r4, a):
        off = a * UNIT
        num4, den4, m4 = _attn_unit(q4_ref[r4, off:off + UNIT, :], kb4[r4, off:off + 2 * UNIT, :],
                                    vb4[r4, off:off + 2 * UNIT, :], bias_ref[1, int(a == 0)])
        rows = slice(r4 * CLS4 + off, r4 * CLS4 + off + UNIT)
        m1, m16 = st[2, rows, :], st[5, rows, :]
        mm = jnp.maximum(jnp.maximum(m1, m4), m16)
        w1, w4, w16 = jnp.exp2(m1 - mm), jnp.exp2(m4 - mm), jnp.exp2(m16 - mm)
        den = w1 * st[1, rows, :] + w4 * den4 + w16 * st[4, rows, :]
        num = w1 * st[0, rows, :] + w4 * num4 + w16 * st[3, rows, :]
        nat_ref[pl.ds(4 * off + r4, UNIT, stride=4), :] = num / den

    def unit1(u):
        q0 = u * Q1_ROWS
        k0 = UNIT + q0 - Q1_ROWS
        q = jnp.concatenate([q4_ref[r4, q0:q0 + Q1_ROWS, :] for r4 in range(4)], axis=0)
        kk = jnp.concatenate([kb4[r4, k0:k0 + K1_ROWS, :] for r4 in range(4)], axis=0)
        vv = jnp.concatenate([vb4[r4, k0:k0 + K1_ROWS, :] for r4 in range(4)], axis=0)
        num, den, mx = _attn_unit(q, kk, vv, bias_ref[0, int(u == 0)])
        for r4 in range(4):
            base = r4 * CLS4 + u * 32
            rows = slice(r4 * 32, (r4 + 1) * 32)
            st[0, base:base + 32, :] = num[rows]
            st[1, base:base + 32, :] = den[rows]
            st[2, base:base + 32, :] = mx[rows]

    def unit16(r16):
        num, den, mx = _attn_unit(q16_ref[r16], kb16[r16], vb16[r16], bias_ref[2, 1])
        base = (r16 % 4) * CLS4 + r16 // 4
        st[3, pl.ds(base, UNIT, stride=4), :] = num
        st[4, pl.ds(base, UNIT, stride=4), :] = den
        st[5, pl.ds(base, UNIT, stride=4), :] = mx

    n_halves = 2
    units1_per_half = ATTN_TILE // UNIT // n_halves
    units4_per_half = CLS4 // UNIT // n_halves
    for half in range(n_halves):
        for u in range(half * units1_per_half, (half + 1) * units1_per_half):
            unit1(u)
        for r4 in range(4):
            if half == 0:
                for c in range(4):
                    unit16(r4 + 4 * c)
            for a in range(half * units4_per_half, (half + 1) * units4_per_half):
                unit4_merge(r4, a)
    o_ref[...] = nat_ref[...].astype(BF16)


def _attention(slopes, q4, k4, v4, q16, k16, v16):
    B, _, n_attn = q4.shape[:3]
    S = n_attn * ATTN_TILE
    s4_spec = pl.BlockSpec((None, None, None, 4, CLS4, HEAD_DIM), lambda b, h, t: (b, h, t, 0, 0, 0))
    s16_spec = pl.BlockSpec((None, None, None, 16, UNIT, HEAD_DIM), lambda b, h, t: (b, h, t, 0, 0, 0))
    return pl.pallas_call(
        _attention_body,
        grid=(B, N_HEADS, n_attn),
        in_specs=[pl.BlockSpec(memory_space=pltpu.SMEM),
                  s4_spec, s4_spec, s4_spec, s16_spec, s16_spec, s16_spec],
        out_specs=pl.BlockSpec((None, ATTN_TILE, HEAD_DIM), lambda b, h, t: (b, t, h)),
        out_shape=jax.ShapeDtypeStruct((B, S, ATTN_WIDTH), BF16),
        scratch_shapes=[
            pltpu.VMEM((4, KB4_ROWS, HEAD_DIM), BF16),
            pltpu.VMEM((4, KB4_ROWS, 2 * HEAD_DIM), BF16),
            pltpu.VMEM((16, KB16_ROWS, HEAD_DIM), BF16),
            pltpu.VMEM((16, KB16_ROWS, 2 * HEAD_DIM), BF16),
            pltpu.VMEM((6, ATTN_TILE, HEAD_DIM), F32),
            pltpu.VMEM((3, 2, UNIT, 2 * UNIT), F32),
            pltpu.VMEM((ATTN_TILE, HEAD_DIM), F32),
        ],
        compiler_params=pltpu.CompilerParams(
            dimension_semantics=("arbitrary", "arbitrary", "arbitrary"),
            vmem_limit_bytes=VMEM_LIMIT_BYTES),
        name="dilated_attention",
    )(slopes, q4, k4, v4, q16, k16, v16)


def _out_proj_body(x_ref, yp_ref, ya_ref, ga_ref, w_ref, gm_ref, x1_ref, h_ref):
    o = ya_ref[...].astype(F32)
    ya = (o * _rms_scale(o, ATTN_WIDTH) * ga_ref[...]).astype(BF16)
    z = jnp.dot(yp_ref[...], w_ref[0:POOL_WIDTH, :], preferred_element_type=F32)
    z = z + jnp.dot(ya, w_ref[POOL_WIDTH:, :], preferred_element_type=F32)
    x1 = x_ref[...] + z
    x1_ref[...] = x1
    h_ref[...] = (x1 * _rms_scale(x1, D_MODEL) * gm_ref[...]).astype(BF16)


def _out_proj(x, y_pool, y_attn, g_attn, w_out, g_mlp):
    B, S, _ = x.shape
    tm = TM_OUT
    const = lambda b, i: (0, 0)
    row = lambda width: pl.BlockSpec((None, tm, width), lambda b, i: (b, i, 0))
    return pl.pallas_call(
        _out_proj_body,
        grid=(B, S // tm),
        in_specs=[row(D_MODEL), row(POOL_WIDTH), row(ATTN_WIDTH),
                  pl.BlockSpec((1, ATTN_WIDTH), const),
                  pl.BlockSpec((POOL_WIDTH + ATTN_WIDTH, D_MODEL), const, pipeline_mode=pl.Buffered(1)),
                  pl.BlockSpec((1, D_MODEL), const)],
        out_specs=[row(D_MODEL), row(D_MODEL)],
        out_shape=[jax.ShapeDtypeStruct((B, S, D_MODEL), F32),
                   jax.ShapeDtypeStruct((B, S, D_MODEL), BF16)],
        compiler_params=pltpu.CompilerParams(
            dimension_semantics=("arbitrary", "arbitrary"), vmem_limit_bytes=VMEM_LIMIT_BYTES),
        name="out_proj",
    )(x, y_pool, y_attn, g_attn, w_out, g_mlp)


def _mlp_body(h_ref, x1_ref, wu_ref, wd_ref, gf_ref, o_ref):
    f = pl.program_id(1)

    def chunk():
        a = jnp.maximum(jnp.dot(h_ref[...], wu_ref[...], preferred_element_type=F32), 0.0)
        return jnp.dot((a * a).astype(BF16), wd_ref[...], preferred_element_type=F32)

    @pl.when(f == 0)
    def _():
        o_ref[...] = x1_ref[...] + chunk()

    @pl.when(f > 0)
    def _():
        o_ref[...] += chunk()

    @pl.when(f == pl.num_programs(1) - 1)
    def _():
        x2 = o_ref[...]
        o_ref[...] = x2 * _rms_scale(x2, D_MODEL) * gf_ref[...]


def _mlp(h, x1, w_up, w_down, g_final):
    M = h.shape[0]
    tm, tf = TM_MLP, TF_MLP
    return pl.pallas_call(
        _mlp_body,
        grid=(M // tm, D_FF // tf),
        in_specs=[pl.BlockSpec((tm, D_MODEL), lambda i, f: (i, 0)),
                  pl.BlockSpec((tm, D_MODEL), lambda i, f: (i, 0)),
                  pl.BlockSpec((D_MODEL, tf), lambda i, f: (0, f)),
                  pl.BlockSpec((tf, D_MODEL), lambda i, f: (f, 0)),
                  pl.BlockSpec((1, D_MODEL), lambda i, f: (0, 0))],
        out_specs=pl.BlockSpec((tm, D_MODEL), lambda i, f: (i, 0)),
        out_shape=jax.ShapeDtypeStruct((M, D_MODEL), F32),
        compiler_params=pltpu.CompilerParams(
            dimension_semantics=("arbitrary", "arbitrary"), vmem_limit_bytes=VMEM_LIMIT_BYTES),
        name="mlp",
    )(h, x1, w_up, w_down, g_final)


def kernel(x, norm_mix_g, w_in, pool_w, pool_scale, pool_out_norm_g, attn_out_norm_g,
           w_out, norm_mlp_g, w_up, w_down, norm_final_g):
    B, S, D = x.shape
    assert D == D_MODEL and S % ATTN_TILE == 0 and norm_mix_g.shape[0] == 1
    slopes = 2.0 ** (-8.0 * jnp.arange(1, N_HEADS + 1, dtype=F32) / N_HEADS)
    y_pool, q4, k4, v4, q16, k16, v16 = _in_proj(
        x, norm_mix_g[0][None, :], w_in[0].astype(BF16), pool_w[0].astype(BF16),
        pool_scale[0][None, :], pool_out_norm_g[0][None, :])
    y_attn = _attention(slopes, q4, k4, v4, q16, k16, v16)
    x1, h = _out_proj(x, y_pool, y_attn, attn_out_norm_g[0][None, :], w_out[0].astype(BF16),
                      norm_mlp_g[0][None, :])
    out = _mlp(h.reshape(B * S, D), x1.reshape(B * S, D), w_up[0].astype(BF16),
               w_down[0].astype(BF16), norm_final_g[None, :])
    return out.reshape(B, S, D)
```

```python
import jax
import jax.numpy as jnp
from jax import lax
from jax.experimental import pallas as pl
from jax.experimental.pallas import tpu as pltpu

F32 = jnp.float32
BF16 = jnp.bfloat16

D_MODEL = 2048
POOL_WIDTH = 1024
POOL_WINDOWS = (2, 4, 8, 16)
POOL_GROUP = 256
MAX_POOL_WINDOW = 16
ATTN_WIDTH = 1024
HEAD_DIM = 128
N_HEADS = 8
DILATIONS = (1, 4, 16)
WINDOW_STEPS = 128
ATTN_TILE = 2048
D_FF = 8192
NORM_EPS = 1e-6
LOG2_E = 1.4426950408889634

VMEM_LIMIT_BYTES = 63 * 1024 * 1024

TM_IN = 512
TM_OUT = 512
TM_MLP = 1024
TF_MLP = 1024


def _rms_scale(v, width):
    return lax.rsqrt(jnp.sum(v * v, axis=-1, keepdims=True) * (1.0 / width) + NORM_EPS)


def _in_proj_body(x_ref, g_ref, w_ref, pw_ref, ps_ref, pg_ref,
                  ypool_ref, q4_ref, k4_ref, v4_ref, q16_ref, k16_ref, v16_ref,
                  ext_ref, nat_ref):
    tm = TM_IN
    i = pl.program_id(1)

    @pl.when(i == 0)
    def _():
        ext_ref[0:MAX_POOL_WINDOW, :] = jnp.zeros((MAX_POOL_WINDOW, POOL_WIDTH), F32)

    halves = []
    for r0 in range(0, tm, tm // 2):
        xr = x_ref[r0:r0 + tm // 2, :]
        hr = (xr * _rms_scale(xr, D_MODEL) * g_ref[...]).astype(BF16)
        ext_ref[MAX_POOL_WINDOW + r0:MAX_POOL_WINDOW + r0 + tm // 2, :] = jnp.dot(
            hr, w_ref[:, 0:POOL_WIDTH], preferred_element_type=F32)
        halves.append(hr)
    h = jnp.concatenate(halves, axis=0)

    def pool_mixer():
        pos1 = (i * tm + lax.broadcasted_iota(jnp.int32, (tm, 1), 0) + 1).astype(F32)
        ys = []
        for g, w in enumerate(POOL_WINDOWS):
            cols = slice(g * POOL_GROUP, (g + 1) * POOL_GROUP)
            tr = ext_ref[:, cols]
            shift = 1
            while shift < w:
                tr = tr + pltpu.roll(tr, shift, 0)
                shift *= 2
            inv_cnt = 1.0 / jnp.minimum(pos1, float(w))
            d = tr[MAX_POOL_WINDOW:, :] * inv_cnt - ext_ref[MAX_POOL_WINDOW:, cols]
            y = jnp.dot(d.astype(BF16), pw_ref[g], preferred_element_type=F32)
            ys.append(y * ps_ref[:, cols])
        ext_ref[0:MAX_POOL_WINDOW, :] = ext_ref[tm:, :]
        ssq = jnp.sum(ys[0] * ys[0], axis=-1, keepdims=True)
        for y in ys[1:]:
            ssq = ssq + jnp.sum(y * y, axis=-1, keepdims=True)
        r = lax.rsqrt(ssq * (1.0 / POOL_WIDTH) + NORM_EPS)
        for g in range(len(POOL_WINDOWS)):
            cols = slice(g * POOL_GROUP, (g + 1) * POOL_GROUP)
            ypool_ref[:, cols] = (ys[g] * r * pg_ref[:, cols]).astype(BF16)

    def qkv(c, o4_ref, o16_ref):
        lo = POOL_WIDTH + c * ATTN_WIDTH
        pr = jnp.dot(h, w_ref[:, lo:lo + ATTN_WIDTH], preferred_element_type=F32)
        if c == 0:
            pr = pr * (HEAD_DIM ** -0.5 * LOG2_E)
        for hh in range(N_HEADS):
            blk = pr[:, hh * HEAD_DIM:(hh + 1) * HEAD_DIM]
            nat_ref[hh] = blk
            o16_ref[hh] = pltpu.einshape("(is)d->sid", blk, s=16).astype(BF16)
        for hh in range(N_HEADS):
            for r4 in range(4):
                o4_ref[hh, r4] = nat_ref[hh, pl.ds(r4, tm // 4, stride=4), :].astype(BF16)

    qkv(0, q4_ref, q16_ref)
    qkv(1, k4_ref, k16_ref)
    pool_mixer()
    qkv(2, v4_ref, v16_ref)


def _in_proj(x, g, w_in, pool_w, pool_scale, pool_g):
    B, S, _ = x.shape
    tm = TM_IN
    tiles_per_attn = ATTN_TILE // tm
    n_attn = S // ATTN_TILE
    const = lambda b, i: (0, 0)
    o4_spec = pl.BlockSpec((None, N_HEADS, None, 4, tm // 4, HEAD_DIM),
                           lambda b, i: (b, 0, i // tiles_per_attn, 0, i % tiles_per_attn, 0))
    o16_spec = pl.BlockSpec((None, N_HEADS, None, 16, tm // 16, HEAD_DIM),
                            lambda b, i: (b, 0, i // tiles_per_attn, 0, i % tiles_per_attn, 0))
    s4 =jax.ShapeDtypeStruct((B, N_HEADS, n_attn, 4, ATTN_TILE // 4, HEAD_DIM), BF16)
    s16 = jax.ShapeDtypeStruct((B, N_HEADS, n_attn, 16, ATTN_TILE // 16, HEAD_DIM), BF16)
    return pl.pallas_call(
        _in_proj_body,
        grid=(B, S // tm),
        in_specs=[
            pl.BlockSpec((None, tm, D_MODEL), lambda b, i: (b, i, 0)),
            pl.BlockSpec((1, D_MODEL), const),
            pl.BlockSpec((D_MODEL, POOL_WIDTH + 3 * ATTN_WIDTH), const, pipeline_mode=pl.Buffered(1)),
            pl.BlockSpec((len(POOL_WINDOWS), POOL_GROUP, POOL_GROUP), lambda b, i: (0, 0, 0),
                         pipeline_mode=pl.Buffered(1)),
            pl.BlockSpec((1, POOL_WIDTH), const),
            pl.BlockSpec((1, POOL_WIDTH), const),
        ],
        out_specs=[
            pl.BlockSpec((None, tm, POOL_WIDTH), lambda b, i: (b, i, 0)),
            o4_spec, o4_spec, o4_spec, o16_spec, o16_spec, o16_spec,
        ],
        out_shape=[jax.ShapeDtypeStruct((B, S, POOL_WIDTH), BF16), s4, s4, s4, s16, s16, s16],
        scratch_shapes=[
            pltpu.VMEM((tm + MAX_POOL_WINDOW, POOL_WIDTH), F32),
            pltpu.VMEM((N_HEADS, tm, HEAD_DIM), F32),
        ],
        compiler_params=pltpu.CompilerParams(
            dimension_semantics=("arbitrary", "arbitrary"), vmem_limit_bytes=VMEM_LIMIT_BYTES),
        name="in_proj",
    )(x, g, w_in, pool_w, pool_scale, pool_g)


KB4_ROWS = WINDOW_STEPS + ATTN_TILE // 4
KB16_ROWS = WINDOW_STEPS + ATTN_TILE // 16
UNIT = 128
CLS4 = ATTN_TILE // 4
Q1_ROWS = UNIT // 4
K1_ROWS = 2 * Q1_ROWS


def _attn_unit(q, kk, vv, bias):
    s = lax.dot_general(q, kk, (((1,), (1,)), ((), ())), preferred_element_type=F32) + bias
    m = jnp.max(s, axis=-1, keepdims=True)
    p = jnp.exp2(s - m)
    acc = jnp.dot(p.astype(BF16), vv, preferred_element_type=F32)
    return acc[:, :HEAD_DIM], acc[:, HEAD_DIM:], jnp.broadcast_to(m, (UNIT, HEAD_DIM))


def _attention_body(slopes_ref, q4_ref, k4_ref, v4_ref, q16_ref, k16_ref, v16_ref, o_ref,
                    kb4, vb4, kb16, vb16, st, bias_ref, nat_ref):
    h = pl.program_id(1)
    t = pl.program_id(2)
    ninf = -jnp.inf

    @pl.when(t == 0)
    def _():
        kb4[:, 0:UNIT, :] = jnp.zeros((4, UNIT, HEAD_DIM), BF16)
        kb16[:, 0:UNIT, :] = jnp.zeros((16, UNIT, HEAD_DIM), BF16)
        vb4[:, 0:UNIT, 0:HEAD_DIM] = jnp.zeros((4, UNIT, HEAD_DIM), BF16)
        vb16[:, 0:UNIT, 0:HEAD_DIM] = jnp.zeros((16, UNIT, HEAD_DIM), BF16)
        vb4[:, :, HEAD_DIM:] = jnp.ones((4, KB4_ROWS, HEAD_DIM), BF16)
        vb16[:, :, HEAD_DIM:] = jnp.ones((16, KB16_ROWS, HEAD_DIM), BF16)

    @pl.when(t > 0)
    def _():
        kb4[:, 0:UNIT, :] = kb4[:, CLS4:, :]
        vb4[:, 0:UNIT, 0:HEAD_DIM] = vb4[:, CLS4:, 0:HEAD_DIM]
        kb16[:, 0:UNIT, :] = kb16[:, UNIT:, :]
        vb16[:, 0:UNIT, 0:HEAD_DIM] = vb16[:, UNIT:, 0:HEAD_DIM]

    kb4[:, UNIT:, :] = k4_ref[...]
    vb4[:, UNIT:, 0:HEAD_DIM] = v4_ref[...]
    kb16[:, UNIT:, :] = k16_ref[...]
    vb16[:, UNIT:, 0:HEAD_DIM] = v16_ref[...]

    slope = slopes_ref[h]
    row = lax.broadcasted_iota(jnp.int32, (UNIT, 2 * UNIT), 0)
    col = lax.broadcasted_iota(jnp.int32, (UNIT, 2 * UNIT), 1)
    for p, dil in enumerate(DILATIONS):
        if dil == 1:
            j = 4 * (row % Q1_ROWS) + row // Q1_ROWS - 4 * (col % K1_ROWS) - col // K1_ROWS + UNIT
            lookback = (col % K1_ROWS) < Q1_ROWS
        else:
            j = row + UNIT - col
            lookback = col < UNIT
        valid = (j >= 0) & (j <= WINDOW_STEPS)
        b = jnp.where(valid, -(slope * (float(dil) * LOG2_E)) * j.astype(F32), ninf)
        bias_ref[p, 0] = b
        bias_ref[p, 1] = b + jnp.where(lookback & (t == 0), ninf, 0.0).astype(F32)

    def unit4_merge(r4, a):
        off = a * UNIT
        num4, den4, m4 = _attn_unit(q4_ref[r4, off:off + UNIT, :], kb4[r4, off:off + 2 * UNIT, :],
                                    vb4[r4, off:off + 2 * UNIT, :], bias_ref[1, int(a == 0)])
        rows = slice(r4 * CLS4 + off, r4 * CLS4 + off + UNIT)
        m1, m16 = st[2, rows, :], st[5, rows, :]
        mm = jnp.maximum(jnp.maximum(m1, m4), m16)
        w1, w4, w16 = jnp.exp2(m1 - mm), jnp.exp2(m4 - mm), jnp.exp2(m16 - mm)
        den = w1 * st[1, rows, :] + w4 * den4 + w16 * st[4, rows, :]
        num = w1 * st[0, rows, :] + w4 * num4 + w16 * st[3, rows, :]
        nat_ref[pl.ds(4 * off + r4, UNIT, stride=4), :] = num / den

    def unit1(u):
        q0 = u * Q1_ROWS
        k0 = UNIT + q0 - Q1_ROWS
        q = jnp.concatenate([q4_ref[r4, q0:q0 + Q1_ROWS, :] for r4 in range(4)], axis=0)
        kk = jnp.concatenate([kb4[r4, k0:k0 + K1_ROWS, :] for r4 in range(4)], axis=0)
        vv = jnp.concatenate([vb4[r4, k0:k0 + K1_ROWS, :] for r4 in range(4)], axis=0)
        num, den, mx = _attn_unit(q, kk, vv, bias_ref[0, int(u == 0)])
        for r4 in range(4):
            base = r4 * CLS4 + u * 32
            rows = slice(r4 * 32, (r4 + 1) * 32)
            st[0, base:base + 32, :] = num[rows]
            st[1, base:base + 32, :] = den[rows]
            st[2, base:base + 32, :] = mx[rows]

    def unit16(r16):
        num, den, mx = _attn_unit(q16_ref[r16], kb16[r16], vb16[r16], bias_ref[2, 1])
        base = (r16 % 4) * CLS4 + r16 // 4
        st[3, pl.ds(base, UNIT, stride=4), :] = num
        st[4, pl.ds(base, UNIT, stride=4), :] = den
        st[5, pl.ds(base, UNIT, stride=4), :] = mx

    n_halves = 2
    units1_per_half = ATTN_TILE // UNIT // n_halves
    units4_per_half = CLS4 // UNIT // n_halves
    for half in range(n_halves):
        for u in range(half * units1_per_half, (half + 1) * units1_per_half):
            unit1(u)
        for r4 in range(4):
            if half == 0:
                for c in range(4):
                    unit16(r4 + 4 * c)
            for a in range(half * units4_per_half, (half + 1) * units4_per_half):
                unit4_merge(r4, a)
    o_ref[...] = nat_ref[...].astype(BF16)


def _attention(slopes, q4, k4, v4, q16, k16, v16):
    B, _, n_attn = q4.shape[:3]
    S = n_attn * ATTN_TILE
    s4_spec = pl.BlockSpec((None, None, None, 4, CLS4, HEAD_DIM), lambda b, h, t: (b, h, t, 0, 0, 0))
    s16_spec = pl.BlockSpec((None, None, None, 16, UNIT, HEAD_DIM), lambda b, h, t: (b, h, t, 0, 0, 0))
    return pl.pallas_call(
        _attention_body,
        grid=(B, N_HEADS, n_attn),
        in_specs=[pl.BlockSpec(memory_space=pltpu.SMEM),
                  s4_spec, s4_spec, s4_spec, s16_spec, s16_spec, s16_spec],
        out_specs=pl.BlockSpec((None, ATTN_TILE, HEAD_DIM), lambda b, h, t: (b, t, h)),
        out_shape=jax.ShapeDtypeStruct((B, S, ATTN_WIDTH), BF16),
        scratch_shapes=[
            pltpu.VMEM((4, KB4_ROWS, HEAD_DIM), BF16),
            pltpu.VMEM((4, KB4_ROWS, 2 * HEAD_DIM), BF16),
            pltpu.VMEM((16, KB16_ROWS, HEAD_DIM), BF16),
            pltpu.VMEM((16, KB16_ROWS, 2 * HEAD_DIM), BF16),
            pltpu.VMEM((6, ATTN_TILE, HEAD_DIM), F32),
            pltpu.VMEM((3, 2, UNIT, 2 * UNIT), F32),
            pltpu.VMEM((ATTN_TILE, HEAD_DIM), F32),
        ],
        compiler_params=pltpu.CompilerParams(
            dimension_semantics=("arbitrary", "arbitrary", "arbitrary"),
            vmem_limit_bytes=VMEM_LIMIT_BYTES),
        name="dilated_attention",
    )(slopes, q4, k4, v4, q16, k16, v16)


def _out_proj_body(x_ref, yp_ref, ya_ref, ga_ref, w_ref, gm_ref, x1_ref, h_ref):
    o = ya_ref[...].astype(F32)
    ya = (o * _rms_scale(o, ATTN_WIDTH) * ga_ref[...]).astype(BF16)
    z = jnp.dot(yp_ref[...], w_ref[0:POOL_WIDTH, :], preferred_element_type=F32)
    z = z + jnp.dot(ya, w_ref[POOL_WIDTH:, :], preferred_element_type=F32)
    x1 = x_ref[...] + z
    x1_ref[...] = x1
    h_ref[...] = (x1 * _rms_scale(x1, D_MODEL) * gm_ref[...]).astype(BF16)


def _out_proj(x, y_pool, y_attn, g_attn, w_out, g_mlp):
    B, S, _ = x.shape
    tm = TM_OUT
    const = lambda b, i: (0, 0)
    row = lambda width: pl.BlockSpec((None, tm, width), lambda b, i: (b, i, 0))
    return pl.pallas_call(
        _out_proj_body,
        grid=(B, S // tm),
        in_specs=[row(D_MODEL), row(POOL_WIDTH), row(ATTN_WIDTH),
                  pl.BlockSpec((1, ATTN_WIDTH), const),
                  pl.BlockSpec((POOL_WIDTH + ATTN_WIDTH, D_MODEL), const, pipeline_mode=pl.Buffered(1)),
                  pl.BlockSpec((1, D_MODEL), const)],
        out_specs=[row(D_MODEL), row(D_MODEL)],
        out_shape=[jax.ShapeDtypeStruct((B, S, D_MODEL), F32),
                   jax.ShapeDtypeStruct((B, S, D_MODEL), BF16)],
        compiler_params=pltpu.CompilerParams(
            dimension_semantics=("arbitrary", "arbitrary"), vmem_limit_bytes=VMEM_LIMIT_BYTES),
        name="out_proj",
    )(x, y_pool, y_attn, g_attn, w_out, g_mlp)


def _mlp_body(h_ref, x1_ref, wu_ref, wd_ref, gf_ref, o_ref):
    f = pl.program_id(1)

    def chunk():
        a = jnp.maximum(jnp.dot(h_ref[...], wu_ref[...], preferred_element_type=F32), 0.0)
        return jnp.dot((a * a).astype(BF16), wd_ref[...], preferred_element_type=F32)

    @pl.when(f == 0)
    def _():
        o_ref[...] = x1_ref[...] + chunk()

    @pl.when(f > 0)
    def _():
        o_ref[...] += chunk()

    @pl.when(f == pl.num_programs(1) - 1)
    def _():
        x2 = o_ref[...]
        o_ref[...] = x2 * _rms_scale(x2, D_MODEL) * gf_ref[...]


def _mlp(h, x1, w_up, w_down, g_final):
    M = h.shape[0]
    tm, tf = TM_MLP, TF_MLP
    return pl.pallas_call(
        _mlp_body,
        grid=(M // tm, D_FF // tf),
        in_specs=[pl.BlockSpec((tm, D_MODEL), lambda i, f: (i, 0)),
                  pl.BlockSpec((tm, D_MODEL), lambda i, f: (i, 0)),
                  pl.BlockSpec((D_MODEL, tf), lambda i, f: (0, f)),
                  pl.BlockSpec((tf, D_MODEL), lambda i, f: (f, 0)),
                  pl.BlockSpec((1, D_MODEL), lambda i, f: (0, 0))],
        out_specs=pl.BlockSpec((tm, D_MODEL), lambda i, f: (i, 0)),
        out_shape=jax.ShapeDtypeStruct((M, D_MODEL), F32),
        compiler_params=pltpu.CompilerParams(
            dimension_semantics=("arbitrary", "arbitrary"), vmem_limit_bytes=VMEM_LIMIT_BYTES),
        name="mlp",
    )(h, x1, w_up, w_down, g_final)


def kernel(x, norm_mix_g, w_in, pool_w, pool_scale, pool_out_norm_g, attn_out_norm_g,
           w_out, norm_mlp_g, w_up, w_down, norm_final_g):
    B, S, D = x.shape
    assert D == D_MODEL and S % ATTN_TILE == 0 and norm_mix_g.shape[0] == 1
    slopes = 2.0 ** (-8.0 * jnp.arange(1, N_HEADS + 1, dtype=F32) / N_HEADS)
    y_pool, q4, k4, v4, q16, k16, v16 = _in_proj(
        x, norm_mix_g[0][None, :], w_in[0].astype(BF16), pool_w[0].astype(BF16),
        pool_scale[0][None, :], pool_out_norm_g[0][None, :])
    y_attn = _attention(slopes, q4, k4, v4, q16, k16, v16)
    x1, h = _out_proj(x, y_pool, y_attn, attn_out_norm_g[0][None, :], w_out[0].astype(BF16),
                      norm_mlp_g[0][None, :])
    out = _mlp(h.reshape(B * S, D), x1.reshape(B * S, D), w_up[0].astype(BF16),
               w_down[0].astype(BF16), norm_final_g[None, :])
    return out.reshape(B, S, D)
```

```python
import jax
import jax.numpy as jnp
from jax import lax
from jax.experimental import pallas as pl
from jax.experimental.pallas import tpu as pltpu

F32 = jnp.float32
BF16 = jnp.bfloat16

D_MODEL = 2048
POOL_WIDTH = 1024
POOL_WINDOWS = (2, 4, 8, 16)
POOL_GROUP = 256
MAX_POOL_WINDOW = 16
ATTN_WIDTH = 1024
HEAD_DIM = 128
N_HEADS = 8
DILATIONS = (1, 4, 16)
WINDOW_STEPS = 128
ATTN_TILE = 2048
D_FF = 8192
NORM_EPS = 1e-6
LOG2_E = 1.4426950408889634

VMEM_LIMIT_BYTES = 63 * 1024 * 1024

TM_IN = 512
TM_OUT = 512
TM_MLP = 1024
TF_MLP = 1024


def _rms_scale(v, width):
    return lax.rsqrt(jnp.sum(v * v, axis=-1, keepdims=True) * (1.0 / width) + NORM_EPS)


def _in_proj_body(x_ref, g_ref, w_ref, pw_ref, ps_ref, pg_ref,
                  ypool_ref, q4_ref, k4_ref, v4_ref, q16_ref, k16_ref, v16_ref,
                  ext_ref, nat_ref):
    tm = TM_IN
    i = pl.program_id(1)

    @pl.when(i == 0)
    def _():
        ext_ref[0:MAX_POOL_WINDOW, :] = jnp.zeros((MAX_POOL_WINDOW, POOL_WIDTH), F32)

    halves = []
    for r0 in range(0, tm, tm // 2):
        xr = x_ref[r0:r0 + tm // 2, :]
        hr = (xr * _rms_scale(xr, D_MODEL) * g_ref[...]).astype(BF16)
        ext_ref[MAX_POOL_WINDOW + r0:MAX_POOL_WINDOW + r0 + tm // 2, :] = jnp.dot(
            hr, w_ref[:, 0:POOL_WIDTH], preferred_element_type=F32)
        halves.append(hr)
    h = jnp.concatenate(halves, axis=0)

    def pool_mixer():
        pos1 = (i * tm + lax.broadcasted_iota(jnp.int32, (tm, 1), 0) + 1).astype(F32)
        ys = []
        for g, w in enumerate(POOL_WINDOWS):
            cols = slice(g * POOL_GROUP, (g + 1) * POOL_GROUP)
            tr = ext_ref[:, cols]
            shift = 1
            while shift < w:
                tr = tr + pltpu.roll(tr, shift, 0)
                shift *= 2
            inv_cnt = 1.0 / jnp.minimum(pos1, float(w))
            d = tr[MAX_POOL_WINDOW:, :] * inv_cnt - ext_ref[MAX_POOL_WINDOW:, cols]
            y = jnp.dot(d.astype(BF16), pw_ref[g], preferred_element_type=F32)
            ys.append(y * ps_ref[:, cols])
        ext_ref[0:MAX_POOL_WINDOW, :] = ext_ref[tm:, :]
        ssq = jnp.sum(ys[0] * ys[0], axis=-1, keepdims=True)
        for y in ys[1:]:
            ssq = ssq + jnp.sum(y * y, axis=-1, keepdims=True)
        r = lax.rsqrt(ssq * (1.0 / POOL_WIDTH) + NORM_EPS)
        for g in range(len(POOL_WINDOWS)):
            cols = slice(g * POOL_GROUP, (g + 1) * POOL_GROUP)
            ypool_ref[:, cols] = (ys[g] * r * pg_ref[:, cols]).astype(BF16)

    def qkv(c, o4_ref, o16_ref):
        lo = POOL_WIDTH + c * ATTN_WIDTH
        pr = jnp.dot(h, w_ref[:, lo:lo + ATTN_WIDTH], preferred_element_type=F32)
        if c == 0:
            pr = pr * (HEAD_DIM ** -0.5 * LOG2_E)
        for hh in range(N_HEADS):
            blk = pr[:, hh * HEAD_DIM:(hh + 1) * HEAD_DIM]
            nat_ref[hh] = blk
            o16_ref[hh] = pltpu.einshape("(is)d->sid", blk, s=16).astype(BF16)
        for hh in range(N_HEADS):
            for r4 in range(4):
                o4_ref[hh, r4] = nat_ref[hh, pl.ds(r4, tm // 4, stride=4), :].astype(BF16)

    qkv(0, q4_ref, q16_ref)
    qkv(1, k4_ref, k16_ref)
    pool_mixer()
    qkv(2, v4_ref, v16_ref)


def _in_proj(x, g, w_in, pool_w, pool_scale, pool_g):
    B, S, _ = x.shape
    tm = TM_IN
    tiles_per_attn = ATTN_TILE // tm
    n_attn = S // ATTN_TILE
    const = lambda b, i: (0, 0)
    o4_spec = pl.BlockSpec((None, N_HEADS, None, 4, tm // 4, HEAD_DIM),
                           lambda b, i: (b, 0, i // tiles_per_attn, 0, i % tiles_per_attn, 0))
    o16_spec = pl.BlockSpec((None, N_HEADS, None, 16, tm // 16, HEAD_DIM),
                            lambda b, i: (b, 0, i // tiles_per_attn, 0, i % tiles_per_attn, 0))
    s4 =jax.ShapeDtypeStruct((B, N_HEADS, n_attn, 4, ATTN_TILE // 4, HEAD_DIM), BF16)
    s16 = jax.ShapeDtypeStruct((B, N_HEADS, n_attn, 16, ATTN_TILE // 16, HEAD_DIM), BF16)
    return pl.pallas_call(
        _in_proj_body,
        grid=(B, S // tm),
        in_specs=[
            pl.BlockSpec((None, tm, D_MODEL), lambda b, i: (b, i, 0)),
            pl.BlockSpec((1, D_MODEL), const),
            pl.BlockSpec((D_MODEL, POOL_WIDTH + 3 * ATTN_WIDTH), const, pipeline_mode=pl.Buffered(1)),
            pl.BlockSpec((len(POOL_WINDOWS), POOL_GROUP, POOL_GROUP), lambda b, i: (0, 0, 0),
                         pipeline_mode=pl.Buffered(1)),
            pl.BlockSpec((1, POOL_WIDTH), const),
            pl.BlockSpec((1, POOL_WIDTH), const),
        ],
        out_specs=[
            pl.BlockSpec((None, tm, POOL_WIDTH), lambda b, i: (b, i, 0)),
            o4_spec, o4_spec, o4_spec, o16_spec, o16_spec, o16_spec,
        ],
        out_shape=[jax.ShapeDtypeStruct((B, S, POOL_WIDTH), BF16), s4, s4, s4, s16, s16, s16],
        scratch_shapes=[
            pltpu.VMEM((tm + MAX_POOL_WINDOW, POOL_WIDTH), F32),
            pltpu.VMEM((N_HEADS, tm, HEAD_DIM), F32),
        ],
        compiler_params=pltpu.CompilerParams(
            dimension_semantics=("arbitrary", "arbitrary"), vmem_limit_bytes=VMEM_LIMIT_BYTES),
        name="in_proj",
    )(x, g, w_in, pool_w, pool_scale, pool_g)


UNIT = 128
CLS4 = ATTN_TILE // 4
Q1_ROWS = UNIT // 4
K1_ROWS = 2 * Q1_ROWS


def _attn_unit(q, kk, vv, bias):
    s = lax.dot_general(q, kk, (((1,), (1,)), ((), ())), preferred_element_type=F32) + bias
    m = jnp.max(s, axis=-1, keepdims=True)
    p = jnp.exp2(s - m)
    acc = jnp.dot(p.astype(BF16), vv, preferred_element_type=F32)
    return acc[:, :HEAD_DIM], acc[:, HEAD_DIM:], jnp.broadcast_to(m, (UNIT, HEAD_DIM))


def _attention_body(slopes_ref, q4_ref, k4_ref, v4_ref, q16_ref, k16_ref, v16_ref, o_ref,
                    kp4, vp4, kp16, vp16, st, bias_ref, nat_ref):
    h = pl.program_id(1)
    t = pl.program_id(2)
    ninf = -jnp.inf

    @pl.when(t == 0)
    def _():
        kp4[...] = jnp.zeros(kp4.shape, BF16)
        vp4[...] = jnp.zeros(vp4.shape, BF16)
        kp16[...] = jnp.zeros(kp16.shape, BF16)
        vp16[...] = jnp.zeros(vp16.shape, BF16)

    def window(prev_ref, cur_ref, cls, lo, n):
        pieces = []
        if lo < UNIT:
            pieces.append(prev_ref[cls, lo:min(UNIT, lo + n), :])
        if lo + n > UNIT:
            pieces.append(cur_ref[cls, max(lo, UNIT) - UNIT:lo + n - UNIT, :])
        return pieces[0] if len(pieces) == 1 else jnp.concatenate(pieces, axis=0)

    def with_ones(v):
        return jnp.concatenate([v, jnp.ones(v.shape, BF16)], axis=-1)

    slope = slopes_ref[h]
    row = lax.broadcasted_iota(jnp.int32, (UNIT, 2 * UNIT), 0)
    col = lax.broadcasted_iota(jnp.int32, (UNIT, 2 * UNIT), 1)
    for p, dil in enumerate(DILATIONS):
        if dil == 1:
            j = 4 * (row % Q1_ROWS) + row // Q1_ROWS - 4 * (col % K1_ROWS) - col // K1_ROWS + UNIT
            lookback = (col % K1_ROWS) < Q1_ROWS
        else:
            j = row + UNIT - col
            lookback = col < UNIT
        valid = (j >= 0) & (j <= WINDOW_STEPS)
        b = jnp.where(valid, -(slope * (float(dil) * LOG2_E)) * j.astype(F32), ninf)
        bias_ref[p, 0] = b
        bias_ref[p, 1] = b + jnp.where(lookback & (t == 0), ninf, 0.0).astype(F32)

    def unit4_merge(r4, a):
        off = a * UNIT
        num4, den4, m4 = _attn_unit(q4_ref[r4, off:off + UNIT, :],
                                    window(kp4, k4_ref, r4, off, 2 * UNIT),
                                    with_ones(window(vp4, v4_ref, r4, off, 2 * UNIT)),
                                    bias_ref[1, int(a == 0)])
        rows = slice(r4 * CLS4 + off, r4 * CLS4 + off + UNIT)
        m1, m16 = st[2, rows, :], st[5, rows, :]
        mm = jnp.maximum(jnp.maximum(m1, m4), m16)
        w1, w4, w16 = jnp.exp2(m1 - mm), jnp.exp2(m4 - mm), jnp.exp2(m16 - mm)
        den = w1 * st[1, rows, :] + w4 * den4 + w16 * st[4, rows, :]
        num = w1 * st[0, rows, :] + w4 * num4 + w16 * st[3, rows, :]
        nat_ref[pl.ds(4 * off + r4, UNIT, stride=4), :] = num / den

    def unit1(u):
        q0 = u * Q1_ROWS
        k0 = UNIT + q0 - Q1_ROWS
        q = jnp.concatenate([q4_ref[r4, q0:q0 + Q1_ROWS, :] for r4 in range(4)], axis=0)
        kk = jnp.concatenate([window(kp4, k4_ref, r4, k0, K1_ROWS) for r4 in range(4)], axis=0)
        vv = jnp.concatenate([window(vp4, v4_ref, r4, k0, K1_ROWS) for r4 in range(4)], axis=0)
        num, den, mx = _attn_unit(q, kk, with_ones(vv), bias_ref[0, int(u == 0)])
        for r4 in range(4):
            base = r4 * CLS4 + u * 32
            rows = slice(r4 * 32, (r4 + 1) * 32)
            st[0, base:base + 32, :] = num[rows]
            st[1, base:base + 32, :] = den[rows]
            st[2, base:base + 32, :] = mx[rows]

    def unit16(r16):
        num, den, mx = _attn_unit(q16_ref[r16], window(kp16, k16_ref, r16, 0, 2 * UNIT),
                                  with_ones(window(vp16, v16_ref, r16, 0, 2 * UNIT)), bias_ref[2, 1])
        base = (r16 % 4) * CLS4 + r16 // 4
        st[3, pl.ds(base, UNIT, stride=4), :] = num
        st[4, pl.ds(base, UNIT, stride=4), :] = den
        st[5, pl.ds(base, UNIT, stride=4), :] = mx

    n_halves = 2
    units1_per_half = ATTN_TILE // UNIT // n_halves
    units4_per_half = CLS4 // UNIT // n_halves
    for half in range(n_halves):
        for u in range(half * units1_per_half, (half + 1) * units1_per_half):
            unit1(u)
        for r4 in range(4):
            if half == 0:
                for c in range(4):
                    unit16(r4 + 4 * c)
            for a in range(half * units4_per_half, (half + 1) * units4_per_half):
                unit4_merge(r4, a)
    o_ref[...] = nat_ref[...].astype(BF16)

    @pl.when(t + 1 < pl.num_programs(2))
    def _():
        kp4[...] = k4_ref[:, CLS4 - UNIT:, :]
        vp4[...] = v4_ref[:, CLS4 - UNIT:, :]
        kp16[...] = k16_ref[...]
        vp16[...] = v16_ref[...]


def _attention(slopes, q4, k4, v4, q16, k16, v16):
    B, _, n_attn = q4.shape[:3]
    S = n_attn * ATTN_TILE
    s4_spec = pl.BlockSpec((None, None, None, 4, CLS4, HEAD_DIM), lambda b, h, t: (b, h, t, 0, 0, 0))
    s16_spec = pl.BlockSpec((None, None, None, 16, UNIT, HEAD_DIM), lambda b, h, t: (b, h, t, 0, 0, 0))
    return pl.pallas_call(
        _attention_body,
        grid=(B, N_HEADS, n_attn),
        in_specs=[pl.BlockSpec(memory_space=pltpu.SMEM),
                  s4_spec, s4_spec, s4_spec, s16_spec, s16_spec, s16_spec],
        out_specs=pl.BlockSpec((None, ATTN_TILE, HEAD_DIM), lambda b, h, t: (b, t, h)),
        out_shape=jax.ShapeDtypeStruct((B, S, ATTN_WIDTH), BF16),
        scratch_shapes=[
            pltpu.VMEM((4, UNIT, HEAD_DIM), BF16),
            pltpu.VMEM((4, UNIT, HEAD_DIM), BF16),
            pltpu.VMEM((16, UNIT, HEAD_DIM), BF16),
            pltpu.VMEM((16, UNIT, HEAD_DIM), BF16),
            pltpu.VMEM((6, ATTN_TILE, HEAD_DIM), F32),
            pltpu.VMEM((3, 2, UNIT, 2 * UNIT), F32),
            pltpu.VMEM((ATTN_TILE, HEAD_DIM), F32),
        ],
        compiler_params=pltpu.CompilerParams(
            dimension_semantics=("arbitrary", "arbitrary", "arbitrary"),
            vmem_limit_bytes=VMEM_LIMIT_BYTES),
        name="dilated_attention",
    )(slopes, q4, k4, v4, q16, k16, v16)


def _out_proj_body(x_ref, yp_ref, ya_ref, ga_ref, w_ref, gm_ref, x1_ref, h_ref):
    o = ya_ref[...].astype(F32)
    ya = (o * _rms_scale(o, ATTN_WIDTH) * ga_ref[...]).astype(BF16)
    z = jnp.dot(yp_ref[...], w_ref[0:POOL_WIDTH, :], preferred_element_type=F32)
    z = z + jnp.dot(ya, w_ref[POOL_WIDTH:, :], preferred_element_type=F32)
    x1 = x_ref[...] + z
    x1_ref[...] = x1
    h_ref[...] = (x1 * _rms_scale(x1, D_MODEL) * gm_ref[...]).astype(BF16)


def _out_proj(x, y_pool, y_attn, g_attn, w_out, g_mlp):
    B, S, _ = x.shape
    tm = TM_OUT
    const = lambda b, i: (0, 0)
    row = lambda width: pl.BlockSpec((None, tm, width), lambda b, i: (b, i, 0))
    return pl.pallas_call(
        _out_proj_body,
        grid=(B, S // tm),
        in_specs=[row(D_MODEL), row(POOL_WIDTH), row(ATTN_WIDTH),
                  pl.BlockSpec((1, ATTN_WIDTH), const),
                  pl.BlockSpec((POOL_WIDTH + ATTN_WIDTH, D_MODEL), const, pipeline_mode=pl.Buffered(1)),
                  pl.BlockSpec((1, D_MODEL), const)],
        out_specs=[row(D_MODEL), row(D_MODEL)],
        out_shape=[jax.ShapeDtypeStruct((B, S, D_MODEL), F32),
                   jax.ShapeDtypeStruct((B, S, D_MODEL), BF16)],
        compiler_params=pltpu.CompilerParams(
            dimension_semantics=("arbitrary", "arbitrary"), vmem_limit_bytes=VMEM_LIMIT_BYTES),
        name="out_proj",
    )(x, y_pool, y_attn, g_attn, w_out, g_mlp)


def _mlp_body(h_ref, x1_ref, wu_ref, wd_ref, gf_ref, o_ref):
    f = pl.program_id(1)

    def chunk():
        a = jnp.maximum(jnp.dot(h_ref[...], wu_ref[...], preferred_element_type=F32), 0.0)
        return jnp.dot((a * a).astype(BF16), wd_ref[...], preferred_element_type=F32)

    @pl.when(f == 0)
    def _():
        o_ref[...] = x1_ref[...] + chunk()

    @pl.when(f > 0)
    def _():
        o_ref[...] += chunk()

    @pl.when(f == pl.num_programs(1) - 1)
    def _():
        x2 = o_ref[...]
        o_ref[...] = x2 * _rms_scale(x2, D_MODEL) * gf_ref[...]


def _mlp(h, x1, w_up, w_down, g_final):
    M = h.shape[0]
    tm, tf = TM_MLP, TF_MLP
    return pl.pallas_call(
        _mlp_body,
        grid=(M // tm, D_FF // tf),
        in_specs=[pl.BlockSpec((tm, D_MODEL), lambda i, f: (i, 0)),
                  pl.BlockSpec((tm, D_MODEL), lambda i, f: (i, 0)),
                  pl.BlockSpec((D_MODEL, tf), lambda i, f: (0, f)),
                  pl.BlockSpec((tf, D_MODEL), lambda i, f: (f, 0)),
                  pl.BlockSpec((1, D_MODEL), lambda i, f: (0, 0))],
        out_specs=pl.BlockSpec((tm, D_MODEL), lambda i, f: (i, 0)),
        out_shape=jax.ShapeDtypeStruct((M, D_MODEL), F32),
        compiler_params=pltpu.CompilerParams(
            dimension_semantics=("arbitrary", "arbitrary"), vmem_limit_bytes=VMEM_LIMIT_BYTES),
        name="mlp",
    )(h, x1, w_up, w_down, g_final)


def kernel(x, norm_mix_g, w_in, pool_w, pool_scale, pool_out_norm_g, attn_out_norm_g,
           w_out, norm_mlp_g, w_up, w_down, norm_final_g):
    B, S, D = x.shape
    assert D == D_MODEL and S % ATTN_TILE == 0 and norm_mix_g.shape[0] == 1
    slopes = 2.0 ** (-8.0 * jnp.arange(1, N_HEADS + 1, dtype=F32) / N_HEADS)
    y_pool, q4, k4, v4, q16, k16, v16 = _in_proj(
        x, norm_mix_g[0][None, :], w_in[0].astype(BF16), pool_w[0].astype(BF16),
        pool_scale[0][None, :], pool_out_norm_g[0][None, :])
    y_attn = _attention(slopes, q4, k4, v4, q16, k16, v16)
    x1, h = _out_proj(x, y_pool, y_attn, attn_out_norm_g[0][None, :], w_out[0].astype(BF16),
                      norm_mlp_g[0][None, :])
    out = _mlp(h.reshape(B * S, D), x1.reshape(B * S, D), w_up[0].astype(BF16),
               w_down[0].astype(BF16), norm_final_g[None, :])
    return out.reshape(B, S, D)
```

```python
import jax
import jax.numpy as jnp
from jax import lax
from jax.experimental import pallas as pl
from jax.experimental.pallas import tpu as pltpu

F32 = jnp.float32
BF16 = jnp.bfloat16

D_MODEL = 2048
POOL_WIDTH = 1024
POOL_WINDOWS = (2, 4, 8, 16)
POOL_GROUP = 256
MAX_POOL_WINDOW = 16
ATTN_WIDTH = 1024
HEAD_DIM = 128
N_HEADS = 8
DILATIONS = (1, 4, 16)
WINDOW_STEPS = 128
ATTN_TILE = 2048
D_FF = 8192
NORM_EPS = 1e-6
LOG2_E = 1.4426950408889634

VMEM_LIMIT_BYTES = 63 * 1024 * 1024

TM_IN = 512
TM_OUT = 512
TM_MLP = 1024
TF_MLP = 1024


def _rms_scale(v, width):
    return lax.rsqrt(jnp.sum(v * v, axis=-1, keepdims=True) * (1.0 / width) + NORM_EPS)


def _in_proj_body(x_ref, g_ref, w_ref, pw_ref, ps_ref, pg_ref,
                  ypool_ref, q4_ref, k4_ref, v4_ref, q16_ref, k16_ref, v16_ref,
                  ext_ref, nat_ref):
    tm = TM_IN
    i = pl.program_id(1)

    @pl.when(i == 0)
    def _():
        ext_ref[0:MAX_POOL_WINDOW, :] = jnp.zeros((MAX_POOL_WINDOW, POOL_WIDTH), F32)

    halves = []
    for r0 in range(0, tm, tm // 2):
        xr = x_ref[r0:r0 + tm // 2, :]
        hr = (xr * _rms_scale(xr, D_MODEL) * g_ref[...]).astype(BF16)
        ext_ref[MAX_POOL_WINDOW + r0:MAX_POOL_WINDOW + r0 + tm // 2, :] = jnp.dot(
            hr, w_ref[:, 0:POOL_WIDTH], preferred_element_type=F32)
        halves.append(hr)
    h = jnp.concatenate(halves, axis=0)

    def pool_mixer():
        pos1 = (i * tm + lax.broadcasted_iota(jnp.int32, (tm, 1), 0) + 1).astype(F32)
        ys = []
        for g, w in enumerate(POOL_WINDOWS):
            cols = slice(g * POOL_GROUP, (g + 1) * POOL_GROUP)
            tr = ext_ref[:, cols]
            shift = 1
            while shift < w:
                tr = tr + pltpu.roll(tr, shift, 0)
                shift *= 2
            inv_cnt = 1.0 / jnp.minimum(pos1, float(w))
            d = tr[MAX_POOL_WINDOW:, :] * inv_cnt - ext_ref[MAX_POOL_WINDOW:, cols]
            y = jnp.dot(d.astype(BF16), pw_ref[g], preferred_element_type=F32)
            ys.append(y * ps_ref[:, cols])
        ext_ref[0:MAX_POOL_WINDOW, :] = ext_ref[tm:, :]
        ssq = jnp.sum(ys[0] * ys[0], axis=-1, keepdims=True)
        for y in ys[1:]:
            ssq = ssq + jnp.sum(y * y, axis=-1, keepdims=True)
        r = lax.rsqrt(ssq * (1.0 / POOL_WIDTH) + NORM_EPS)
        for g in range(len(POOL_WINDOWS)):
            cols = slice(g * POOL_GROUP, (g + 1) * POOL_GROUP)
            ypool_ref[:, cols] = (ys[g] * r * pg_ref[:, cols]).astype(BF16)

    def qkv(c, o4_ref, o16_ref):
        lo = POOL_WIDTH + c * ATTN_WIDTH
        pr = jnp.dot(h, w_ref[:, lo:lo + ATTN_WIDTH], preferred_element_type=F32)
        if c == 0:
            pr = pr * (HEAD_DIM ** -0.5 * LOG2_E)
        for hh in range(N_HEADS):
            blk = pr[:, hh * HEAD_DIM:(hh + 1) * HEAD_DIM]
            nat_ref[hh] = blk
            o16_ref[hh] = pltpu.einshape("(is)d->sid", blk, s=16).astype(BF16)
        for hh in range(N_HEADS):
            for r4 in range(4):
                o4_ref[hh, r4] = nat_ref[hh, pl.ds(r4, tm // 4, stride=4), :].astype(BF16)

    qkv(0, q4_ref, q16_ref)
    qkv(1, k4_ref, k16_ref)
    pool_mixer()
    qkv(2, v4_ref, v16_ref)


def _in_proj(x, g, w_in, pool_w, pool_scale, pool_g):
    B, S, _ = x.shape
    tm = TM_IN
    tiles_per_attn = ATTN_TILE // tm
    n_attn = S // ATTN_TILE
    const = lambda b, i: (0, 0)
    o4_spec = pl.BlockSpec((None, N_HEADS, None, 4, tm // 4, HEAD_DIM),
                           lambda b, i: (b, 0, i // tiles_per_attn, 0, i % tiles_per_attn, 0))
    o16_spec = pl.BlockSpec((None, N_HEADS, None, 16, tm // 16, HEAD_DIM),
                            lambda b, i: (b, 0, i // tiles_per_attn, 0, i % tiles_per_attn, 0))
    s4 =jax.ShapeDtypeStruct((B, N_HEADS, n_attn, 4, ATTN_TILE // 4, HEAD_DIM), BF16)
    s16 = jax.ShapeDtypeStruct((B, N_HEADS, n_attn, 16, ATTN_TILE // 16, HEAD_DIM), BF16)
    return pl.pallas_call(
        _in_proj_body,
        grid=(B, S // tm),
        in_specs=[
            pl.BlockSpec((None, tm, D_MODEL), lambda b, i: (b, i, 0)),
            pl.BlockSpec((1, D_MODEL), const),
            pl.BlockSpec((D_MODEL, POOL_WIDTH + 3 * ATTN_WIDTH), const, pipeline_mode=pl.Buffered(1)),
            pl.BlockSpec((len(POOL_WINDOWS), POOL_GROUP, POOL_GROUP), lambda b, i: (0, 0, 0),
                         pipeline_mode=pl.Buffered(1)),
            pl.BlockSpec((1, POOL_WIDTH), const),
            pl.BlockSpec((1, POOL_WIDTH), const),
        ],
        out_specs=[
            pl.BlockSpec((None, tm, POOL_WIDTH), lambda b, i: (b, i, 0)),
            o4_spec, o4_spec, o4_spec, o16_spec, o16_spec, o16_spec,
        ],
        out_shape=[jax.ShapeDtypeStruct((B, S, POOL_WIDTH), BF16), s4, s4, s4, s16, s16, s16],
        scratch_shapes=[
            pltpu.VMEM((tm + MAX_POOL_WINDOW, POOL_WIDTH), F32),
            pltpu.VMEM((N_HEADS, tm, HEAD_DIM), F32),
        ],
        compiler_params=pltpu.CompilerParams(
            dimension_semantics=("arbitrary", "arbitrary"), vmem_limit_bytes=VMEM_LIMIT_BYTES),
        name="in_proj",
    )(x, g, w_in, pool_w, pool_scale, pool_g)


UNIT = 128
CLS4 = ATTN_TILE // 4
Q1_ROWS = UNIT // 4
K1_ROWS = 2 * Q1_ROWS


def _attn_unit(q, kk, vv, bias):
    s = lax.dot_general(q, kk, (((1,), (1,)), ((), ())), preferred_element_type=F32) + bias
    m = jnp.max(s, axis=-1, keepdims=True)
    p = jnp.exp2(s - m)
    acc = jnp.dot(p.astype(BF16), vv, preferred_element_type=F32)
    return acc[:, :HEAD_DIM], acc[:, HEAD_DIM:], jnp.broadcast_to(m, (UNIT, HEAD_DIM))


def _attention_body(slopes_ref, q4_ref, k4_ref, v4_ref, kp4, vp4, q16_ref, k16_ref, v16_ref, kp16, vp16,
                    o_ref, st, bias_ref, nat_ref):
    h = pl.program_id(1)
    t = pl.program_id(2)
    ninf = -jnp.inf

    def window(prev_ref, cur_ref, cls, lo, n):
        pieces = []
        if lo < UNIT:
            pieces.append(prev_ref[cls, lo:min(UNIT, lo + n), :])
        if lo + n > UNIT:
            pieces.append(cur_ref[cls, max(lo, UNIT) - UNIT:lo + n - UNIT, :])
        return pieces[0] if len(pieces) == 1 else jnp.concatenate(pieces, axis=0)

    def with_ones(v):
        return jnp.concatenate([v, jnp.ones(v.shape, BF16)], axis=-1)

    slope = slopes_ref[h]
    row = lax.broadcasted_iota(jnp.int32, (UNIT, 2 * UNIT), 0)
    col = lax.broadcasted_iota(jnp.int32, (UNIT, 2 * UNIT), 1)
    for p, dil in enumerate(DILATIONS):
        if dil == 1:
            j = 4 * (row % Q1_ROWS) + row // Q1_ROWS - 4 * (col % K1_ROWS) - col // K1_ROWS + UNIT
            lookback = (col % K1_ROWS) < Q1_ROWS
        else:
            j = row + UNIT - col
            lookback = col < UNIT
        valid = (j >= 0) & (j <= WINDOW_STEPS)
        b = jnp.where(valid, -(slope * (float(dil) * LOG2_E)) * j.astype(F32), ninf)
        bias_ref[p, 0] = b
        bias_ref[p, 1] = b + jnp.where(lookback & (t == 0), ninf, 0.0).astype(F32)

    def unit4_merge(r4, a):
        off = a * UNIT
        num4, den4, m4 = _attn_unit(q4_ref[r4, off:off + UNIT, :],
                                    window(kp4, k4_ref, r4, off, 2 * UNIT),
                                    with_ones(window(vp4, v4_ref, r4, off, 2 * UNIT)),
                                    bias_ref[1, int(a == 0)])
        rows = slice(r4 * CLS4 + off, r4 * CLS4 + off + UNIT)
        m1, m16 = st[2, rows, :], st[5, rows, :]
        mm = jnp.maximum(jnp.maximum(m1, m4), m16)
        w1, w4, w16 = jnp.exp2(m1 - mm), jnp.exp2(m4 - mm), jnp.exp2(m16 - mm)
        den = w1 * st[1, rows, :] + w4 * den4 + w16 * st[4, rows, :]
        num = w1 * st[0, rows, :] + w4 * num4 + w16 * st[3, rows, :]
        nat_ref[pl.ds(4 * off + r4, UNIT, stride=4), :] = num / den

    def unit1(u):
        q0 = u * Q1_ROWS
        k0 = UNIT + q0 - Q1_ROWS
        q = jnp.concatenate([q4_ref[r4, q0:q0 + Q1_ROWS, :] for r4 in range(4)], axis=0)
        kk = jnp.concatenate([window(kp4, k4_ref, r4, k0, K1_ROWS) for r4 in range(4)], axis=0)
        vv = jnp.concatenate([window(vp4, v4_ref, r4, k0, K1_ROWS) for r4 in range(4)], axis=0)
        num, den, mx = _attn_unit(q, kk, with_ones(vv), bias_ref[0, int(u == 0)])
        for r4 in range(4):
            base = r4 * CLS4 + u * 32
            rows = slice(r4 * 32, (r4 + 1) * 32)
            st[0, base:base + 32, :] = num[rows]
            st[1, base:base + 32, :] = den[rows]
            st[2, base:base + 32, :] = mx[rows]

    def unit16(r16):
        num, den, mx = _attn_unit(q16_ref[r16], window(kp16, k16_ref, r16, 0, 2 * UNIT),
                                  with_ones(window(vp16, v16_ref, r16, 0, 2 * UNIT)), bias_ref[2, 1])
        base = (r16 % 4) * CLS4 + r16 // 4
        st[3, pl.ds(base, UNIT, stride=4), :] = num
        st[4, pl.ds(base, UNIT, stride=4), :] = den
        st[5, pl.ds(base, UNIT, stride=4), :] = mx

    n_halves = 2
    units1_per_half = ATTN_TILE // UNIT // n_halves
    units4_per_half = CLS4 // UNIT // n_halves
    for half in range(n_halves):
        for u in range(half * units1_per_half, (half + 1) * units1_per_half):
            unit1(u)
        for r4 in range(4):
            if half == 0:
                for c in range(4):
                    unit16(r4 + 4 * c)
            for a in range(half * units4_per_half, (half + 1) * units4_per_half):
                unit4_merge(r4, a)
    o_ref[...] = nat_ref[...].astype(BF16)


def _attention(slopes, q4, k4, v4, q16, k16, v16):
    B, _, n_attn = q4.shape[:3]
    S = n_attn * ATTN_TILE
    s4_spec = pl.BlockSpec((None, None, None, 4, CLS4, HEAD_DIM), lambda b, h, t: (b, h, t, 0, 0, 0))
    s16_spec = pl.BlockSpec((None, None, None, 16, UNIT, HEAD_DIM), lambda b, h, t: (b, h, t, 0, 0, 0))
    prev = lambda t: jnp.maximum(t - 1, 0)
    p4_spec = pl.BlockSpec((None, None, None, 4, UNIT, HEAD_DIM),
                           lambda b, h, t: (b, h, prev(t), 0, CLS4 // UNIT - 1, 0))
    p16_spec = pl.BlockSpec((None, None, None, 16, UNIT, HEAD_DIM), lambda b, h, t: (b, h, prev(t), 0, 0, 0))
    return pl.pallas_call(
        _attention_body,
        grid=(B, N_HEADS, n_attn),
        in_specs=[pl.BlockSpec(memory_space=pltpu.SMEM),
                  s4_spec, s4_spec, s4_spec, p4_spec, p4_spec,
                  s16_spec, s16_spec, s16_spec, p16_spec, p16_spec],
        out_specs=pl.BlockSpec((None, ATTN_TILE, HEAD_DIM), lambda b, h, t: (b, t, h)),
        out_shape=jax.ShapeDtypeStruct((B, S, ATTN_WIDTH), BF16),
        scratch_shapes=[
            pltpu.VMEM((6, ATTN_TILE, HEAD_DIM), F32),
            pltpu.VMEM((3, 2, UNIT, 2 * UNIT), F32),
            pltpu.VMEM((ATTN_TILE, HEAD_DIM), F32),
        ],
        compiler_params=pltpu.CompilerParams(
            dimension_semantics=("arbitrary", "arbitrary", "arbitrary"),
            vmem_limit_bytes=VMEM_LIMIT_BYTES),
        name="dilated_attention",
    )(slopes, q4, k4, v4, k4, v4, q16, k16, v16, k16, v16)


def _out_proj_body(x_ref, yp_ref, ya_ref, ga_ref, w_ref, gm_ref, x1_ref, h_ref):
    o = ya_ref[...].astype(F32)
    ya = (o * _rms_scale(o, ATTN_WIDTH) * ga_ref[...]).astype(BF16)
    z = jnp.dot(yp_ref[...], w_ref[0:POOL_WIDTH, :], preferred_element_type=F32)
    z = z + jnp.dot(ya, w_ref[POOL_WIDTH:, :], preferred_element_type=F32)
    x1 = x_ref[...] + z
    x1_ref[...] = x1
    h_ref[...] = (x1 * _rms_scale(x1, D_MODEL) * gm_ref[...]).astype(BF16)


def _out_proj(x, y_pool, y_attn, g_attn, w_out, g_mlp):
    B, S, _ = x.shape
    tm = TM_OUT
    const = lambda b, i: (0, 0)
    row = lambda width: pl.BlockSpec((None, tm, width), lambda b, i: (b, i, 0))
    return pl.pallas_call(
        _out_proj_body,
        grid=(B, S // tm),
        in_specs=[row(D_MODEL), row(POOL_WIDTH), row(ATTN_WIDTH),
                  pl.BlockSpec((1, ATTN_WIDTH), const),
                  pl.BlockSpec((POOL_WIDTH + ATTN_WIDTH, D_MODEL), const, pipeline_mode=pl.Buffered(1)),
                  pl.BlockSpec((1, D_MODEL), const)],
        out_specs=[row(D_MODEL), row(D_MODEL)],
        out_shape=[jax.ShapeDtypeStruct((B, S, D_MODEL), F32),
                   jax.ShapeDtypeStruct((B, S, D_MODEL), BF16)],
        compiler_params=pltpu.CompilerParams(
            dimension_semantics=("arbitrary", "arbitrary"), vmem_limit_bytes=VMEM_LIMIT_BYTES),
        name="out_proj",
    )(x, y_pool, y_attn, g_attn, w_out, g_mlp)


def _mlp_body(h_ref, x1_ref, wu_ref, wd_ref, gf_ref, o_ref):
    f = pl.program_id(1)

    def chunk():
        a = jnp.maximum(jnp.dot(h_ref[...], wu_ref[...], preferred_element_type=F32), 0.0)
        return jnp.dot((a * a).astype(BF16), wd_ref[...], preferred_element_type=F32)

    @pl.when(f == 0)
    def _():
        o_ref[...] = x1_ref[...] + chunk()

    @pl.when(f > 0)
    def _():
        o_ref[...] += chunk()

    @pl.when(f == pl.num_programs(1) - 1)
    def _():
        x2 = o_ref[...]
        o_ref[...] = x2 * _rms_scale(x2, D_MODEL) * gf_ref[...]


def _mlp(h, x1, w_up, w_down, g_final):
    M = h.shape[0]
    tm, tf = TM_MLP, TF_MLP
    return pl.pallas_call(
        _mlp_body,
        grid=(M // tm, D_FF // tf),
        in_specs=[pl.BlockSpec((tm, D_MODEL), lambda i, f: (i, 0)),
                  pl.BlockSpec((tm, D_MODEL), lambda i, f: (i, 0)),
                  pl.BlockSpec((D_MODEL, tf), lambda i, f: (0, f)),
                  pl.BlockSpec((tf, D_MODEL), lambda i, f: (f, 0)),
                  pl.BlockSpec((1, D_MODEL), lambda i, f: (0, 0))],
        out_specs=pl.BlockSpec((tm, D_MODEL), lambda i, f: (i, 0)),
        out_shape=jax.ShapeDtypeStruct((M, D_MODEL), F32),
        compiler_params=pltpu.CompilerParams(
            dimension_semantics=("arbitrary", "arbitrary"), vmem_limit_bytes=VMEM_LIMIT_BYTES),
        name="mlp",
    )(h, x1, w_up, w_down, g_final)


def kernel(x, norm_mix_g, w_in, pool_w, pool_scale, pool_out_norm_g, attn_out_norm_g,
           w_out, norm_mlp_g, w_up, w_down, norm_final_g):
    B, S, D = x.shape
    assert D == D_MODEL and S % ATTN_TILE == 0 and norm_mix_g.shape[0] == 1
    slopes = 2.0 ** (-8.0 * jnp.arange(1, N_HEADS + 1, dtype=F32) / N_HEADS)
    y_pool, q4, k4, v4, q16, k16, v16 = _in_proj(
        x, norm_mix_g[0][None, :], w_in[0].astype(BF16), pool_w[0].astype(BF16),
        pool_scale[0][None, :], pool_out_norm_g[0][None, :])
    y_attn = _attention(slopes, q4, k4, v4, q16, k16, v16)
    x1, h = _out_proj(x, y_pool, y_attn, attn_out_norm_g[0][None, :], w_out[0].astype(BF16),
                      norm_mlp_g[0][None, :])
    out = _mlp(h.reshape(B * S, D), x1.reshape(B * S, D), w_up[0].astype(BF16),
               w_down[0].astype(BF16), norm_final_g[None, :])
    return out.reshape(B, S, D)
```
